```python
import math
import jax, jax.numpy as jnp
from jax import lax
import numpy as np

D_MODEL = 4096
BATCH = 16
SEQ = 256
DEPTH = 1
DEC_BATCH = 2
DEC_SEQ = 1024
PAST_LEN = 256

GRID_W = 64
D_LRU = D_MODEL // 2
LRU_BLOCKS = 16
LRU_BLOCK_W = D_LRU // LRU_BLOCKS
LRU_C = 8.0
LRU_MIN_RAD = 0.9
LRU_MAX_RAD = 0.999
CONV_W = 4
DN_HEADS = 16
DN_DK = D_MODEL // 32
DN_DV = D_MODEL // 32
DN_QK = DN_HEADS * DN_DK
DN_V = DN_HEADS * DN_DV
DN_CONV_CH = 2 * DN_QK + DN_V
DN_CHUNK = 64
D_FF = 4 * D_MODEL
DEEPNORM_ALPHA = (2.0 * DEPTH) ** 0.25
DEEPNORM_BETA = (8.0 * DEPTH) ** -0.25
LN_EPS = 1e-5
RMS_EPS = 1e-6
POS_BASE = 10000.0
IN_WIDTHS = (D_LRU, D_LRU, DN_CONV_CH, DN_V, 2 * DN_HEADS, 2 * DN_HEADS, 2 * D_MODEL)
IN_SPLITS = tuple(int(s) for s in np.cumsum(IN_WIDTHS)[:-1])
N_IN = int(sum(IN_WIDTHS))

kernel_name = 'hawk_deltanet_diffusion_step'

F32 = jnp.float32


def layer_norm(x, g, b):
    xf = x.astype(F32)
    mu = jnp.mean(xf, axis=-1, keepdims=True)
    var = jnp.mean(jnp.square(xf - mu), axis=-1, keepdims=True)
    return ((xf - mu) * lax.rsqrt(var + LN_EPS) * g.astype(F32) + b.astype(F32)).astype(x.dtype)


def l2_normalize(x):
    return x * lax.rsqrt(jnp.sum(jnp.square(x), axis=-1, keepdims=True) + RMS_EPS)


def centred_depthwise_conv(x, w):
    left = CONV_W // 2
    return lax.conv_general_dilated(
        x, w[:, None, :].astype(x.dtype), window_strides=(1,),
        padding=[(left, CONV_W - 1 - left)],
        dimension_numbers=('NWC', 'WIO', 'NWC'),
        feature_group_count=x.shape[-1])


def linear_scan(a, b, h0, reverse):
    def combine(left, right):
        a_l, b_l = left
        a_r, b_r = right
        return a_l * a_r, a_r * b_l + b_r
    a_cum, b_cum = lax.associative_scan(combine, (a, b), axis=1, reverse=reverse)
    return a_cum * h0[:, None, :] + b_cum


def grid_pos_embed(n_tokens):
    rows = n_tokens // GRID_W
    row = jnp.repeat(jnp.arange(rows, dtype=F32), GRID_W)
    col = jnp.tile(jnp.arange(GRID_W, dtype=F32), rows)
    quarter = D_MODEL // 4
    omega = 1.0 / (POS_BASE ** (jnp.arange(quarter, dtype=F32) / quarter))
    er = row[:, None] * omega
    ec = col[:, None] * omega
    return jnp.concatenate([jnp.sin(er), jnp.cos(er), jnp.sin(ec), jnp.cos(ec)], axis=-1)


def chunk_gated_delta(q, k, v, beta, g, s0):
    bsz, t_len = q.shape[0], q.shape[1]
    n_chunks = t_len // DN_CHUNK

    def to_chunks(a):
        a = a.reshape((bsz, n_chunks, DN_CHUNK) + a.shape[2:])
        return jnp.moveaxis(a, (1, 2), (0, 3))

    qc, kc, vc, bc = to_chunks(q), to_chunks(k), to_chunks(v), to_chunks(beta)
    gc = jnp.cumsum(to_chunks(g), axis=-1)
    idx = jnp.arange(DN_CHUNK)
    causal = idx[:, None] >= idx[None, :]
    strict = idx[:, None] > idx[None, :]
    decay = jnp.exp(jnp.where(causal, gc[..., :, None] - gc[..., None, :], -jnp.inf))
    kb = kc * bc[..., None]
    l_mat = jnp.where(strict, jnp.einsum('nbhcd,nbhsd->nbhcs', kb, kc) * decay, 0.0)
    eye = jnp.eye(DN_CHUNK, dtype=F32)
    rhs = jnp.concatenate([vc * bc[..., None], kb * jnp.exp(gc)[..., None]], axis=-1)
    sol = lax.linalg.triangular_solve(l_mat + eye, rhs, left_side=True, lower=True, unit_diagonal=True)
    u, w = sol[..., :DN_DV], sol[..., DN_DV:]
    attn = jnp.einsum('nbhcd,nbhsd->nbhcs', qc, kc) * decay
    g_last = gc[..., -1]
    q_dec = qc * jnp.exp(gc)[..., None]
    k_dec = kc * jnp.exp(g_last[..., None] - gc)[..., None]

    def step(s, xs):
        u_n, w_n, q_n, k_n, a_n, gl_n = xs
        v_new = u_n - jnp.einsum('bhck,bhkv->bhcv', w_n, s)
        o_n = jnp.einsum('bhck,bhkv->bhcv', q_n, s) + jnp.einsum('bhcs,bhsv->bhcv', a_n, v_new)
        s = s * jnp.exp(gl_n)[..., None, None] + jnp.einsum('bhck,bhcv->bhkv', k_n, v_new)
        return s, o_n

    s_fin, o = lax.scan(step, s0, (u, w, q_dec, k_dec, attn, g_last))
    o = jnp.moveaxis(o, (0, 3), (1, 2)).reshape(bsz, t_len, DN_HEADS, DN_DV)
    return o, s_fin


def rg_lru_branch(x_in, y_in, h0, p):
    bsz, t_len, _ = x_in.shape
    xc = (centred_depthwise_conv(x_in, p['lru_conv_w']) + p['lru_conv_b']).astype(F32)
    xb = xc.reshape(bsz, t_len, LRU_BLOCKS, LRU_BLOCK_W)
    gate_pre = jnp.einsum('btnk,dgnkj->dgbtnj', xb, p['lru_gate_w'].astype(F32))
    gate_pre = gate_pre.reshape(2, 2, bsz, t_len, D_LRU)
    gates = jax.nn.sigmoid(gate_pre + p['lru_gate_b'].astype(F32)[:, :, None, None, :])
    r, i = gates[:, 0], gates[:, 1]
    log_a = -LRU_C * r * jax.nn.softplus(-p['lru_lambda'].astype(F32))[:, None, None, :]
    a = jnp.exp(log_a)
    b = jnp.sqrt(-jnp.expm1(2.0 * log_a)) * (i * xc)
    h0 = h0.astype(F32)
    h_f = linear_scan(a[0], b[0], h0[:, 0], reverse=False)
    h_b = linear_scan(a[1], b[1], h0[:, 1], reverse=True)
    state = jnp.stack([h_f[:, -1], h_b[:, 0]], axis=1)
    out = (h_f + h_b) * jax.nn.gelu(y_in.astype(F32))
    return out, state


def gdn_branch(qkv_in, z, beta_logit, a_logit, s0, p):
    bsz, t_len, _ = qkv_in.shape
    qkv = jax.nn.silu(centred_depthwise_conv(qkv_in, p['dn_conv_w']).astype(F32))
    q, k, v = jnp.split(qkv, (DN_QK, 2 * DN_QK), axis=-1)
    q = l2_normalize(q.reshape(bsz, t_len, DN_HEADS, DN_DK)) * (DN_DK ** -0.5)
    k = l2_normalize(k.reshape(bsz, t_len, DN_HEADS, DN_DK))
    v = v.reshape(bsz, t_len, DN_HEADS, DN_DV)
    beta = jax.nn.sigmoid(beta_logit.astype(F32)).reshape(bsz, t_len, 2, DN_HEADS)
    g = -jnp.exp(p['dn_a_log'].astype(F32)) * jax.nn.softplus(
        a_logit.astype(F32).reshape(bsz, t_len, 2, DN_HEADS) + p['dn_dt_bias'].astype(F32))
    s0 = s0.astype(F32)
    o_f, s_f = chunk_gated_delta(q, k, v, beta[:, :, 0], g[:, :, 0], s0[:, 0])
    rev = lambda t: jnp.flip(t, axis=1)
    o_b, s_b = chunk_gated_delta(rev(q), rev(k), rev(v), rev(beta[:, :, 1]), rev(g[:, :, 1]), s0[:, 1])
    o = o_f + rev(o_b)
    o = o * lax.rsqrt(jnp.mean(jnp.square(o), axis=-1, keepdims=True) + RMS_EPS) * p['dn_norm_w'].astype(F32)
    o = o * jax.nn.silu(z.astype(F32).reshape(bsz, t_len, DN_HEADS, DN_DV))
    return o.reshape(bsz, t_len, DN_V), jnp.stack([s_f, s_b], axis=1)


def token_mixers(h, lru_h0, dn_s0, p):
    bsz, t_len, _ = h.shape
    x_lru, y_lru, qkv, z, beta_logit, a_logit, gate_logit = jnp.split(h @ p['w_in'], IN_SPLITS, axis=-1)
    lru_out, lru_state = rg_lru_branch(x_lru, y_lru, lru_h0, p)
    dn_out, dn_state = gdn_branch(qkv, z, beta_logit, a_logit, dn_s0, p)
    gate = jax.nn.sigmoid(gate_logit.reshape(bsz, t_len, 2, D_MODEL) + p['b_branch'])
    merged = (gate[:, :, 0] * (lru_out.astype(h.dtype) @ p['w_lru_proj'])
              + gate[:, :, 1] * (dn_out.astype(h.dtype) @ p['w_dn_proj']))
    return merged @ p['w_o'], lru_state, dn_state


def sq_relu_mlp(h, w_up, w_down):
    return jnp.square(jax.nn.relu(h @ w_up)) @ w_down


def trunk_layer(x, mod, lru_h0, dn_s0, p):
    shift_m, scale_m, gate_m, shift_f, scale_f, gate_f = jnp.split(mod[:, None, :].astype(x.dtype), 6, axis=-1)
    mix, lru_state, dn_state = token_mixers(x * (1 + scale_m) + shift_m, lru_h0, dn_s0, p)
    x = layer_norm(DEEPNORM_ALPHA * x + gate_m * mix, p['ln1_g'], p['ln1_b'])
    ffn = sq_relu_mlp(x * (1 + scale_f) + shift_f, p['w_up'], p['w_down'])
    x = layer_norm(DEEPNORM_ALPHA * x + gate_f * ffn, p['ln2_g'], p['ln2_b'])
    return x, lru_state, dn_state


def setup_inputs(seed: int = 0) -> dict:
    key = jax.random.key(seed)
    ks = jax.random.split(key, 32)
    nrm = lambda k, shape, s: jax.random.normal(k, shape, F32) * s
    x_prompt = nrm(ks[0], (BATCH, SEQ, D_MODEL), 1.0)
    x_sample = nrm(ks[1], (DEC_BATCH, DEC_SEQ, D_MODEL), 1.0)
    state_lru = nrm(ks[2], (DEC_BATCH, DEPTH, 2, D_LRU), 0.5)
    state_dn = nrm(ks[3], (DEC_BATCH, DEPTH, 2, DN_HEADS, DN_DK, DN_DV), 0.1)
    c = nrm(ks[4], (DEC_BATCH, D_MODEL), 1.0)
    c_ctx = nrm(ks[5], (D_MODEL,), 1.0)
    w_mod = nrm(ks[6], (DEPTH, D_MODEL, 6 * D_MODEL), 0.5 * D_MODEL ** -0.5)
    b_mod = nrm(ks[7], (DEPTH, 6 * D_MODEL), 0.02)
    w_in = nrm(ks[8], (DEPTH, D_MODEL, N_IN), D_MODEL ** -0.5)
    lru_conv_w = nrm(ks[9], (DEPTH, CONV_W, D_LRU), CONV_W ** -0.5)
    lru_conv_b = nrm(ks[10], (DEPTH, D_LRU), 0.02)
    lru_gate_w = nrm(ks[11], (DEPTH, 2, 2, LRU_BLOCKS, LRU_BLOCK_W, LRU_BLOCK_W), LRU_BLOCK_W ** -0.5)
    lru_gate_b = nrm(ks[12], (DEPTH, 2, 2, D_LRU), 0.1)
    rad = jax.random.uniform(ks[13], (DEPTH, 2, D_LRU), F32, LRU_MIN_RAD, LRU_MAX_RAD)
    sig = rad ** (1.0 / LRU_C)
    lru_lambda = jnp.log(sig) - jnp.log1p(-sig)
    dn_conv_w = nrm(ks[14], (DEPTH, CONV_W, DN_CONV_CH), CONV_W ** -0.5)
    dn_a_log = jnp.log(jax.random.uniform(ks[15], (DEPTH, 2, DN_HEADS), F32, 1.0, 16.0))
    dt = jnp.exp(jax.random.uniform(ks[16], (DEPTH, 2, DN_HEADS), F32, math.log(1e-3), math.log(1e-1)))
    dn_dt_bias = dt + jnp.log(-jnp.expm1(-dt))
    dn_norm_w = 1.0 + nrm(ks[17], (DEPTH, DN_DV), 0.1)
    b_branch = nrm(ks[18], (DEPTH, 2, D_MODEL), 0.1)
    w_lru_proj = nrm(ks[19], (DEPTH, D_LRU, D_MODEL), DEEPNORM_BETA * D_LRU ** -0.5)
    w_dn_proj = nrm(ks[20], (DEPTH, DN_V, D_MODEL), DEEPNORM_BETA * DN_V ** -0.5)
    w_o = nrm(ks[21], (DEPTH, D_MODEL, D_MODEL), DEEPNORM_BETA * D_MODEL ** -0.5)
    ln1_g = 1.0 + nrm(ks[22], (DEPTH, D_MODEL), 0.1)
    ln1_b = nrm(ks[23], (DEPTH, D_MODEL), 0.02)
    w_up = nrm(ks[24], (DEPTH, D_MODEL, D_FF), D_MODEL ** -0.5)
    w_down = nrm(ks[25], (DEPTH, D_FF, D_MODEL), DEEPNORM_BETA * D_FF ** -0.5)
    ln2_g = 1.0 + nrm(ks[26], (DEPTH, D_MODEL), 0.1)
    ln2_b = nrm(ks[27], (DEPTH, D_MODEL), 0.02)
    return {'x_prompt': x_prompt, 'x_sample': x_sample, 'state_lru': state_lru, 'state_dn': state_dn,
            'c': c, 'c_ctx': c_ctx, 'w_mod': w_mod, 'b_mod': b_mod, 'w_in': w_in,
            'lru_conv_w': lru_conv_w, 'lru_conv_b': lru_conv_b, 'lru_gate_w': lru_gate_w,
            'lru_gate_b': lru_gate_b, 'lru_lambda': lru_lambda, 'dn_conv_w': dn_conv_w,
            'dn_a_log': dn_a_log, 'dn_dt_bias': dn_dt_bias, 'dn_norm_w': dn_norm_w,
            'b_branch': b_branch, 'w_lru_proj': w_lru_proj, 'w_dn_proj': w_dn_proj, 'w_o': w_o,
            'ln1_g': ln1_g, 'ln1_b': ln1_b, 'w_up': w_up, 'w_down': w_down,
            'ln2_g': ln2_g, 'ln2_b': ln2_b}


def reference(x_prompt, x_sample, state_lru, state_dn, c, c_ctx, w_mod, b_mod, w_in,
              lru_conv_w, lru_conv_b, lru_gate_w, lru_gate_b, lru_lambda, dn_conv_w,
              dn_a_log, dn_dt_bias, dn_norm_w, b_branch, w_lru_proj, w_dn_proj, w_o,
              ln1_g, ln1_b, w_up, w_down, ln2_g, ln2_b):
    y_prompt = x_prompt
    y_sample = x_sample + grid_pos_embed(x_sample.shape[1]).astype(x_sample.dtype)[None]
    n_ctx = x_prompt.shape[0]
    zeros_lru = jnp.zeros((n_ctx, 2, D_LRU), F32)
    zeros_dn = jnp.zeros((n_ctx, 2, DN_HEADS, DN_DK, DN_DV), F32)
    new_lru, new_dn = [], []
    for l in range(DEPTH):
        p = {'w_in': w_in[l], 'lru_conv_w': lru_conv_w[l], 'lru_conv_b': lru_conv_b[l],
             'lru_gate_w': lru_gate_w[l], 'lru_gate_b': lru_gate_b[l], 'lru_lambda': lru_lambda[l],
             'dn_conv_w': dn_conv_w[l], 'dn_a_log': dn_a_log[l], 'dn_dt_bias': dn_dt_bias[l],
             'dn_norm_w': dn_norm_w[l], 'b_branch': b_branch[l], 'w_lru_proj': w_lru_proj[l],
             'w_dn_proj': w_dn_proj[l], 'w_o': w_o[l], 'ln1_g': ln1_g[l], 'ln1_b': ln1_b[l],
             'w_up': w_up[l], 'w_down': w_down[l], 'ln2_g': ln2_g[l], 'ln2_b': ln2_b[l]}
        mod_ctx = jax.nn.silu(c_ctx[None, :]) @ w_mod[l] + b_mod[l]
        mod_lat = jax.nn.silu(c) @ w_mod[l] + b_mod[l]
        y_prompt, s_lru, s_dn = trunk_layer(y_prompt, mod_ctx, zeros_lru, zeros_dn, p)
        new_lru.append(s_lru)
        new_dn.append(s_dn)
        y_sample, _, _ = trunk_layer(y_sample, mod_lat, state_lru[:, l], state_dn[:, l], p)
    new_state_lru = jnp.stack(new_lru, axis=1).astype(x_prompt.dtype)
    new_state_dn = jnp.stack(new_dn, axis=1).astype(x_prompt.dtype)
    return (y_prompt, y_sample, new_state_lru, new_state_dn)
```

```python
import functools
import math

import jax
import jax.numpy as jnp
import numpy as np
from jax import lax
from jax.experimental import pallas as pl
from jax.experimental.pallas import tpu as pltpu

F32 = jnp.float32
BF16 = jnp.bfloat16
HIGHEST = lax.Precision.HIGHEST

D_MODEL = 4096
N_CTX_SEQ, CTX_T = 16, 256
N_LAT_SEQ, LAT_T = 2, 1024
M_CTX = N_CTX_SEQ * CTX_T
M_LAT = N_LAT_SEQ * LAT_T
M_TOK = M_CTX + M_LAT
GRID_W = 64
D_LRU = 2048
LRU_C = 8.0
HEADS = 16
HEAD_D = 128
DN_QK = HEADS * HEAD_D
CHUNK = 64
SUPER = 256
D_FF = 4 * D_MODEL
ALPHA = 2.0 ** 0.25
LN_EPS = 1e-5
RMS_EPS = 1e-6
POS_BASE = 10000.0
COL_LRU_X, COL_LRU_Y, COL_Q, COL_K, COL_V, COL_Z = 0, 2048, 4096, 6144, 8192, 10240
COL_BD = 12288
COL_GATE = 12352
N_MAIN = COL_BD
VMEM_LIMIT = 56 * 1024 * 1024


def _cparams(n_axes):
    return pltpu.CompilerParams(dimension_semantics=("arbitrary",) * n_axes, vmem_limit_bytes=VMEM_LIMIT)


def _row_group(i, tm):
    n_ctx = M_CTX // tm
    return jnp.where(i < n_ctx, 0, 1 + (i - n_ctx) // (LAT_T // tm))


def _mod_spec(tm, k, axis):
    if axis == 0:
        return pl.BlockSpec((1, 1, D_MODEL), lambda i: (_row_group(i, tm), 0, k))
    return pl.BlockSpec((1, 1, D_MODEL), lambda j, i: (_row_group(i, tm), 0, k))


def _sigmoid(x):
    return 1.0 / (1.0 + jnp.exp(-x))


def _silu(x):
    return x * _sigmoid(x)


def _softplus(x):
    return jnp.maximum(x, 0.0) + jnp.log1p(jnp.exp(-jnp.abs(x)))


def _mod_kernel(c_ref, w_ref, b_ref, o_ref):
    s = _silu(c_ref[...]).astype(BF16)
    o_ref[...] = jnp.dot(s, w_ref[...].astype(BF16), preferred_element_type=F32) + b_ref[...]


def _modulation(cc, w_mod, b_mod):
    tn = 512
    n = w_mod.shape[1]
    return pl.pallas_call(
        _mod_kernel,
        grid=(n // tn,),
        in_specs=[pl.BlockSpec((8, D_MODEL), lambda j: (0, 0)),
                  pl.BlockSpec((D_MODEL, tn), lambda j: (0, j)),
                  pl.BlockSpec((1, tn), lambda j: (0, j))],
        out_specs=pl.BlockSpec((8, tn), lambda j: (0, j)),
        out_shape=jax.ShapeDtypeStruct((8, n), F32),
        compiler_params=_cparams(1),
        name="modulation",
    )(cc, w_mod, b_mod)


def _token_specs(tm):
    n_ctx = M_CTX // tm
    n_pos = LAT_T // tm
    return [pl.BlockSpec((tm, D_MODEL), lambda i: (jnp.minimum(i, n_ctx - 1), 0)),
            pl.BlockSpec((tm, D_MODEL), lambda i: (jnp.maximum(i - n_ctx, 0), 0)),
            pl.BlockSpec((tm, D_MODEL), lambda i: (jnp.maximum(i - n_ctx, 0) % n_pos, 0))]


def _prep_kernel(xp_ref, xs_ref, pos_ref, sh_ref, sc_ref, h_ref, *, n_ctx_tiles):
    i = pl.program_id(0)
    sc = 1.0 + sc_ref[0]
    sh = sh_ref[0]

    @pl.when(i < n_ctx_tiles)
    def _():
        h_ref[...] = (xp_ref[...] * sc + sh).astype(BF16)

    @pl.when(i >= n_ctx_tiles)
    def _():
        h_ref[...] = ((xs_ref[...] + pos_ref[...]) * sc + sh).astype(BF16)


def _prep(xp, xs, pos, mods3):
    tm = 128
    return pl.pallas_call(
        functools.partial(_prep_kernel, n_ctx_tiles=M_CTX // tm),
        grid=(M_TOK // tm,),
        in_specs=_token_specs(tm) + [_mod_spec(tm, 0, 0), _mod_spec(tm, 1, 0)],
        out_specs=pl.BlockSpec((tm, D_MODEL), lambda i: (i, 0)),
        out_shape=jax.ShapeDtypeStruct((M_TOK, D_MODEL), BF16),
        compiler_params=_cparams(1),
        name="prep",
    )(xp, xs, pos, mods3, mods3)


def _mm_kernel(lhs_ref, w_ref, o_ref, wb_ref, *, relu2):
    @pl.when(pl.program_id(1) == 0)
    def _():
        wb_ref[...] = w_ref[...].astype(BF16)

    acc = jnp.dot(lhs_ref[...], wb_ref[...], preferred_element_type=F32)
    if relu2:
        acc = jnp.maximum(acc, 0.0)
        acc = acc * acc
    o_ref[...] = acc.astype(o_ref.dtype)


def _mm(lhs, w, *, col_block, n_cols, tn, tm, out_dtype, relu2=False, name):
    m, k = lhs.shape
    return pl.pallas_call(
        functools.partial(_mm_kernel, relu2=relu2),
        grid=(n_cols // tn, m // tm),
        in_specs=[pl.BlockSpec((tm, k), lambda j, i: (i, 0)),
                  pl.BlockSpec((k, tn), lambda j, i: (0, j + col_block))],
        out_specs=pl.BlockSpec((tm, tn), lambda j, i: (i, j)),
        out_shape=jax.ShapeDtypeStruct((m, n_cols), out_dtype),
        scratch_shapes=[pltpu.VMEM((k, tn), BF16)],
        compiler_params=_cparams(2),
        name=name,
    )(lhs, w)


def _down_kernel(lhs_ref, w_ref, o_ref, wb_ref, *, tm):
    k = pl.program_id(1)
    i = pl.program_id(2)

    @pl.when(i == 0)
    def _():
        wb_ref[...] = w_ref[...].astype(BF16)

    rows = pl.ds(pl.multiple_of(i * tm, tm), tm)
    acc = jnp.dot(lhs_ref[...], wb_ref[...], preferred_element_type=F32)

    @pl.when(k == 0)
    def _():
        o_ref[rows, :] = acc

    @pl.when(k > 0)
    def _():
        o_ref[rows, :] += acc


def _down(u, w_down):
    tn, tk, tm = 512, 2048, 512
    m = u.shape[0]
    return pl.pallas_call(
        functools.partial(_down_kernel, tm=tm),
        grid=(D_MODEL // tn, D_FF // tk, m // tm),
        in_specs=[pl.BlockSpec((tm, tk), lambda j, k, i: (i, k)),
                  pl.BlockSpec((tk, tn), lambda j, k, i: (k, j))],
        out_specs=pl.BlockSpec((m, tn), lambda j, k, i: (0, j)),
        out_shape=jax.ShapeDtypeStruct((m, D_MODEL), F32),
        scratch_shapes=[pltpu.VMEM((tk, tn), BF16)],
        compiler_params=_cparams(3),
        name="down_proj",
    )(u, w_down)


def _conv4(x, w):
    t = x.shape[0]
    row = lax.broadcasted_iota(jnp.int32, x.shape, 0)
    xm2 = jnp.where(row >= 2, pltpu.roll(x, 2, 0), 0.0)
    xm1 = jnp.where(row >= 1, pltpu.roll(x, 1, 0), 0.0)
    xp1 = jnp.where(row < t - 1, pltpu.roll(x, t - 1, 0), 0.0)
    return w[0:1] * xm2 + w[1:2] * xm1 + w[2:3] * x + w[3:4] * xp1


def _lru_kernel(x_ref, y_ref, cw_ref, cb_ref, gw_ref, gb_ref, lam_ref, h0_ref, o_ref, st_ref,
                af_ref, bf_ref, ab_ref, bb_ref, *, t_len, width):
    for n in range(width // HEAD_D):
        sl = slice(n * HEAD_D, (n + 1) * HEAD_D)
        xc = _conv4(x_ref[:, sl], cw_ref[:, sl]) + cb_ref[:, sl]
        xcb = xc.astype(BF16)
        for d, (a_ref, b_ref) in enumerate(((af_ref, bf_ref), (ab_ref, bb_ref))):
            pre_r = jnp.dot(xcb, gw_ref[d, 0, n].astype(BF16), preferred_element_type=F32)
            pre_i = jnp.dot(xcb, gw_ref[d, 1, n].astype(BF16), preferred_element_type=F32)
            r = _sigmoid(pre_r + gb_ref[2 * d:2 * d + 1, sl])
            ig = _sigmoid(pre_i + gb_ref[2 * d + 1:2 * d + 2, sl])
            log_a = (-LRU_C) * r * _softplus(-lam_ref[d:d + 1, sl])
            a = jnp.exp(log_a)
            mult = jnp.sqrt((1.0 + a * a) * jnp.tanh(-log_a))
            a_ref[:, sl] = a
            b_ref[:, sl] = mult * (ig * xc)

    n_tiles = t_len // 8
    rowi = lax.broadcasted_iota(jnp.int32, (8, width), 0)

    def body(g, carry):
        cf, cb = carry
        r0 = pl.multiple_of(g * 8, 8)
        a8 = af_ref[pl.ds(r0, 8), :]
        b8 = bf_ref[pl.ds(r0, 8), :]
        for dd in (1, 2, 4):
            m = rowi >= dd
            a_sh = jnp.where(m, pltpu.roll(a8, dd, 0), 1.0)
            b_sh = jnp.where(m, pltpu.roll(b8, dd, 0), 0.0)
            b8 = a8 * b_sh + b8
            a8 = a8 * a_sh
        h8 = a8 * cf + b8
        bf_ref[pl.ds(r0, 8), :] = h8
        cf = h8[7:8, :]

        r1 = pl.multiple_of((n_tiles - 1 - g) * 8, 8)
        a8 = ab_ref[pl.ds(r1, 8), :]
        b8 = bb_ref[pl.ds(r1, 8), :]
        for dd in (1, 2, 4):
            m = rowi < 8 - dd
            a_sh = jnp.where(m, pltpu.roll(a8, 8 - dd, 0), 1.0)
            b_sh = jnp.where(m, pltpu.roll(b8, 8 - dd, 0), 0.0)
            b8 = a8 * b_sh + b8
            a8 = a8 * a_sh
        h8 = a8 * cb + b8
        bb_ref[pl.ds(r1, 8), :] = h8
        cb = h8[0:1, :]
        return cf, cb

    cf, cb = lax.fori_loop(0, n_tiles, body, (h0_ref[0, 0:1, :], h0_ref[0, 1:2, :]))
    st_ref[0, 0:1, :] = cf
    st_ref[0, 1:2, :] = cb
    y = y_ref[...]
    gelu = 0.5 * y * (1.0 + jnp.tanh(math.sqrt(2.0 / math.pi) * (y + 0.044715 * (y * y * y))))
    o_ref[...] = ((bf_ref[...] + bb_ref[...]) * gelu).astype(BF16)


def _lru(p_main, conv_w, conv_b, gate_w, gate_b4, lam, h0, *, t_len, row_block0, n_seq):
    width = 256
    nblk = width // HEAD_D
    ncb = D_LRU // width
    return pl.pallas_call(
        functools.partial(_lru_kernel, t_len=t_len, width=width),
        grid=(n_seq, ncb),
        in_specs=[pl.BlockSpec((t_len, width), lambda b, c: (b + row_block0, c)),
                  pl.BlockSpec((t_len, width), lambda b, c: (b + row_block0, c + COL_LRU_Y // width)),
                  pl.BlockSpec((4, width), lambda b, c: (0, c)),
                  pl.BlockSpec((1, width), lambda b, c: (0, c)),
                  pl.BlockSpec((2, 2, nblk, HEAD_D, HEAD_D), lambda b, c: (0, 0, c, 0, 0)),
                  pl.BlockSpec((4, width), lambda b, c: (0, c)),
                  pl.BlockSpec((2, width), lambda b, c: (0, c)),
                  pl.BlockSpec((1, 2, width), lambda b, c: (b, 0, c))],
        out_specs=[pl.BlockSpec((t_len, width), lambda b, c: (b, c)),
                   pl.BlockSpec((1, 2, width), lambda b, c: (b, 0, c))],
        out_shape=[jax.ShapeDtypeStruct((n_seq * t_len, D_LRU), BF16),
                   jax.ShapeDtypeStruct((n_seq, 2, D_LRU), F32)],
        scratch_shapes=[pltpu.VMEM((t_len, width), F32)] * 4,
        compiler_params=_cparams(2),
        name=f"rg_lru_t{t_len}",
    )(p_main, p_main, conv_w, conv_b, gate_w, gate_b4, lam, h0)


def _dn_prepare(q, k, v, beta_b, g_b, upper):
    n = SUPER
    i = lax.broadcasted_iota(jnp.int32, (n, n), 0)
    j = lax.broadcasted_iota(jnp.int32, (n, n), 1)
    same = (i >> 6) == (j >> 6)
    before = (j > i) if upper else (j < i)
    strict = same & before
    incl = same & (before | (i == j))

    g2 = jnp.concatenate([g_b, g_b], axis=1)
    gam = jnp.dot(incl.astype(F32), g2, precision=HIGHEST, preferred_element_type=F32)
    decay = jnp.exp(jnp.where(incl, gam - gam.T, -1e30))

    kb = k * beta_b
    kbf = k.astype(BF16)
    nt = (((1,), (1,)), ((), ()))
    kk = lax.dot_general(kb.astype(BF16), kbf, nt, preferred_element_type=F32)
    lmat = jnp.where(strict, kk * decay, 0.0)
    attn = lax.dot_general(q.astype(BF16), kbf, nt, preferred_element_type=F32) * decay

    minv = (i == j).astype(F32)
    s = 1
    while s < CHUNK:
        il = i & (2 * s - 1)
        jl = j & (2 * s - 1)
        sh = (2 * s).bit_length() - 1
        blk = (i >> sh) == (j >> sh)
        off = blk & ((il < s) & (jl >= s) if upper else (il >= s) & (jl < s))
        c_s = jnp.where(off, lmat, 0.0)
        if s == 1:
            minv = minv - c_s
        else:
            x = jnp.dot(c_s, minv, precision=HIGHEST, preferred_element_type=F32)
            minv = minv - jnp.dot(minv, x, precision=HIGHEST, preferred_element_type=F32)
        s *= 2

    gam1 = gam[:, :HEAD_D]
    eg = jnp.exp(gam1)
    rhs = jnp.concatenate([v * beta_b, kb * eg], axis=1)
    uw = jnp.dot(minv, rhs, precision=HIGHEST, preferred_element_type=F32)
    u, w = uw[:, :HEAD_D], uw[:, HEAD_D:]
    q_dec = q * eg
    lasts = []
    for c in range(n // CHUNK):
        r = c * CHUNK if upper else c * CHUNK + CHUNK - 1
        lasts.append(jnp.broadcast_to(gam1[r:r + 1, :], (CHUNK, HEAD_D)))
    g_last = jnp.concatenate(lasts, axis=0)
    k_dec = k * jnp.exp(g_last - gam1)
    return u, w, attn, q_dec, k_dec, g_last


def _dn_recur(prep, s_ref, vn_ref, upper):
    u, w, attn, q_dec, k_dec, g_last = prep
    vn_ref[...] = jnp.zeros_like(vn_ref)
    n_chunks = SUPER // CHUNK
    outs = [None] * n_chunks
    order = range(n_chunks - 1, -1, -1) if upper else range(n_chunks)
    tn = (((0,), (0,)), ((), ()))
    for c in order:
        rows = slice(c * CHUNK, (c + 1) * CHUNK)
        s = s_ref[...]
        sb = s.astype(BF16)
        v_new = u[rows] - jnp.dot(w[rows].astype(BF16), sb, preferred_element_type=F32)
        vn_ref[rows, :] = v_new
        outs[c] = (jnp.dot(q_dec[rows].astype(BF16), sb, preferred_element_type=F32)
                   + jnp.dot(attn[rows].astype(BF16), vn_ref[...].astype(BF16), preferred_element_type=F32))
        inc = lax.dot_general(k_dec[rows].astype(BF16), v_new.astype(BF16), tn, preferred_element_type=F32)
        s_ref[...] = s * jnp.exp(g_last[c * CHUNK:c * CHUNK + 1, :]) + inc
    return jnp.concatenate(outs, axis=0)


def _gdn_kernel(*refs, t_len, use_state, emit_state):
    (q_ref, k_ref, v_ref, z_ref, bd_ref, cwq_ref, cwk_ref, cwv_ref, alog_ref, dt_ref, nw_ref), refs = refs[:11], refs[11:]
    if use_state:
        s0_ref, refs = refs[0], refs[1:]
    o_ref, refs = refs[0], refs[1:]
    if emit_state:
        st_ref, refs = refs[0], refs[1:]
    qs_ref, ks_ref, vs_ref, bet_ref, g_ref, of_ref, ob_ref, sf_ref, sb_ref, vnf_ref, vnb_ref = refs

    head = pl.program_id(1)

    def l2n(x):
        return x * lax.rsqrt(jnp.sum(x * x, axis=-1, keepdims=True) + RMS_EPS)

    qs_ref[...] = l2n(_silu(_conv4(q_ref[...], cwq_ref[...]))) * (HEAD_D ** -0.5)
    ks_ref[...] = l2n(_silu(_conv4(k_ref[...], cwk_ref[...])))
    vs_ref[...] = _silu(_conv4(v_ref[...], cwv_ref[...]))

    bd = bd_ref[...]
    beta_all = _sigmoid(bd)
    g_all = -jnp.exp(alog_ref[...]) * _softplus(bd + dt_ref[...])
    lane = lax.broadcasted_iota(jnp.int32, bd.shape, 1)
    for d in range(2):
        beta = jnp.sum(jnp.where(lane == d * HEADS + head, beta_all, 0.0), axis=1, keepdims=True)
        g = jnp.sum(jnp.where(lane == 2 * HEADS + d * HEADS + head, g_all, 0.0), axis=1, keepdims=True)
        bet_ref[d] = jnp.broadcast_to(beta, (t_len, HEAD_D))
        g_ref[d] = jnp.broadcast_to(g, (t_len, HEAD_D))

    if use_state:
        sf_ref[...] = s0_ref[0, 0, 0]
        sb_ref[...] = s0_ref[0, 1, 0]
    else:
        sf_ref[...] = jnp.zeros_like(sf_ref)
        sb_ref[...] = jnp.zeros_like(sb_ref)

    n_blocks = t_len // SUPER

    def block_pair(nb):
        for d, (acc_ref, s_ref, vn_ref) in enumerate(((of_ref, sf_ref, vnf_ref), (ob_ref, sb_ref, vnb_ref))):
            blk = nb if d == 0 else n_blocks - 1 - nb
            rows = pl.ds(pl.multiple_of(blk * SUPER, SUPER), SUPER)
            prep = _dn_prepare(qs_ref[rows, :], ks_ref[rows, :], vs_ref[rows, :],
                               bet_ref[d, rows, :], g_ref[d, rows, :], upper=(d == 1))
            acc_ref[rows, :] = _dn_recur(prep, s_ref, vn_ref, upper=(d == 1))

    if n_blocks == 1:
        block_pair(0)
    else:
        def loop_body(nb, carry):
            block_pair(nb)
            return carry
        lax.fori_loop(0, n_blocks, loop_body, 0)

    if emit_state:
        st_ref[0, 0, 0] = sf_ref[...]
        st_ref[0, 1, 0] = sb_ref[...]

    o = of_ref[...] + ob_ref[...]
    o = o * lax.rsqrt(jnp.mean(o * o, axis=-1, keepdims=True) + RMS_EPS) * nw_ref[...]
    o_ref[...] = (o * _silu(z_ref[...])).astype(BF16)


def _gdn(p_main, p_bd, conv_w, alog_vec, dt_vec, norm_w, s0, *, t_len, row_block0, n_seq, emit_state):
    use_state = s0 is not None
    cb = lambda col: col // HEAD_D
    tok = lambda col: pl.BlockSpec((t_len, HEAD_D), lambda b, h: (b + row_block0, h + cb(col)))
    cws = lambda col: pl.BlockSpec((4, HEAD_D), lambda b, h: (0, h + cb(col - COL_Q)))
    vec = pl.BlockSpec((1, HEAD_D), lambda b, h: (0, 0))
    state_spec = pl.BlockSpec((1, 2, 1, HEAD_D, HEAD_D), lambda b, h: (b, 0, h, 0, 0))
    in_specs = [tok(COL_Q), tok(COL_K), tok(COL_V), tok(COL_Z),
                pl.BlockSpec((t_len, HEAD_D), lambda b, h: (b + row_block0, 0)),
                cws(COL_Q), cws(COL_K), cws(COL_V), vec, vec, vec]
    args = [p_main, p_main, p_main, p_main, p_bd, conv_w, conv_w, conv_w, alog_vec, dt_vec, norm_w]
    if use_state:
        in_specs.append(state_spec)
        args.append(s0)
    out_specs = [pl.BlockSpec((t_len, HEAD_D), lambda b, h: (b, h))]
    out_shape = [jax.ShapeDtypeStruct((n_seq * t_len, DN_QK), BF16)]
    if emit_state:
        out_specs.append(state_spec)
        out_shape.append(jax.ShapeDtypeStruct((n_seq, 2, HEADS, HEAD_D, HEAD_D), F32))
    scratch = ([pltpu.VMEM((t_len, HEAD_D), F32)] * 3 + [pltpu.VMEM((2, t_len, HEAD_D), F32)] * 2
               + [pltpu.VMEM((t_len, HEAD_D), F32)] * 2 + [pltpu.VMEM((HEAD_D, HEAD_D), F32)] * 2
               + [pltpu.VMEM((SUPER, HEAD_D), F32)] * 2)
    return pl.pallas_call(
        functools.partial(_gdn_kernel, t_len=t_len, use_state=use_state, emit_state=emit_state),
        grid=(n_seq, HEADS),
        in_specs=in_specs, out_specs=out_specs, out_shape=out_shape,
        scratch_shapes=scratch,
        compiler_params=_cparams(2),
        name=f"gated_delta_t{t_len}",
    )(*args)


def _merge_kernel(a_ref, b_ref, wl_ref, wd_ref, g0_ref, g1_ref, b0_ref, b1_ref, o_ref, wlb_ref, wdb_ref):
    @pl.when(pl.program_id(1) == 0)
    def _():
        wlb_ref[...] = wl_ref[...].astype(BF16)
        wdb_ref[...] = wd_ref[...].astype(BF16)

    pl_ = jnp.dot(a_ref[...], wlb_ref[...], preferred_element_type=F32)
    pd_ = jnp.dot(b_ref[...], wdb_ref[...], preferred_element_type=F32)
    o_ref[...] = (_sigmoid(g0_ref[...] + b0_ref[0]) * pl_ + _sigmoid(g1_ref[...] + b1_ref[0]) * pd_).astype(BF16)


def _merge(lru_out, dn_out, w_lru, w_dn, gates, b_branch3):
    tn, tm = 512, 512
    nj = D_MODEL // tn
    return pl.pallas_call(
        _merge_kernel,
        grid=(nj, M_TOK // tm),
        in_specs=[pl.BlockSpec((tm, D_LRU), lambda j, i: (i, 0)),
                  pl.BlockSpec((tm, DN_QK), lambda j, i: (i, 0)),
                  pl.BlockSpec((D_LRU, tn), lambda j, i: (0, j)),
                  pl.BlockSpec((DN_QK, tn), lambda j, i: (0, j)),
                  pl.BlockSpec((tm, tn), lambda j, i: (i, j)),
                  pl.BlockSpec((tm, tn), lambda j, i: (i, j + nj)),
                  pl.BlockSpec((1, 1, tn), lambda j, i: (0, 0, j)),
                  pl.BlockSpec((1, 1, tn), lambda j, i: (1, 0, j))],
        out_specs=pl.BlockSpec((tm, tn), lambda j, i: (i, j)),
        out_shape=jax.ShapeDtypeStruct((M_TOK, D_MODEL), BF16),
        scratch_shapes=[pltpu.VMEM((D_LRU, tn), BF16), pltpu.VMEM((DN_QK, tn), BF16)],
        compiler_params=_cparams(2),
        name="branch_merge",
    )(lru_out, dn_out, w_lru, w_dn, gates, gates, b_branch3, b_branch3)


def _layer_norm(x, g, b):
    mu = jnp.mean(x, axis=-1, keepdims=True)
    xc = x - mu
    var = jnp.mean(xc * xc, axis=-1, keepdims=True)
    return xc * lax.rsqrt(var + LN_EPS) * g + b


def _ln1_kernel(xp_ref, xs_ref, pos_ref, mix_ref, gm_ref, shf_ref, scf_ref, g_ref, b_ref, x1_ref, h2_ref,
                *, n_ctx_tiles):
    i = pl.program_id(0)

    def finish(x):
        x1 = _layer_norm(ALPHA * x + gm_ref[0] * mix_ref[...], g_ref[...], b_ref[...])
        x1_ref[...] = x1
        h2_ref[...] = (x1 * (1.0 + scf_ref[0]) + shf_ref[0]).astype(BF16)

    @pl.when(i < n_ctx_tiles)
    def _():
        finish(xp_ref[...])

    @pl.when(i >= n_ctx_tiles)
    def _():
        finish(xs_ref[...] + pos_ref[...])


def _ln1(xp, xs, pos, mix, mods3, g, b):
    tm = 128
    row = pl.BlockSpec((tm, D_MODEL), lambda i: (i, 0))
    vec = pl.BlockSpec((1, D_MODEL), lambda i: (0, 0))
    return pl.pallas_call(
        functools.partial(_ln1_kernel, n_ctx_tiles=M_CTX // tm),
        grid=(M_TOK // tm,),
        in_specs=_token_specs(tm) + [row, _mod_spec(tm, 2, 0), _mod_spec(tm, 3, 0), _mod_spec(tm, 4, 0), vec, vec],
        out_specs=[row, row],
        out_shape=[jax.ShapeDtypeStruct((M_TOK, D_MODEL), F32), jax.ShapeDtypeStruct((M_TOK, D_MODEL), BF16)],
        compiler_params=_cparams(1),
        name="residual_ln1",
    )(xp, xs, pos, mix, mods3, mods3, mods3, g, b)


def _ln2_kernel(x1_ref, ffn_ref, gf_ref, g_ref, b_ref, y_ref):
    y_ref[...] = _layer_norm(ALPHA * x1_ref[...] + gf_ref[0] * ffn_ref[...], g_ref[...], b_ref[...])


def _ln2(x1, ffn, mods3, g, b, *, row0, n_rows):
    tm = 128
    t0 = row0 // tm
    row = pl.BlockSpec((tm, D_MODEL), lambda i: (i + t0, 0))
    vec = pl.BlockSpec((1, D_MODEL), lambda i: (0, 0))
    return pl.pallas_call(
        _ln2_kernel,
        grid=(n_rows // tm,),
        in_specs=[row, row, pl.BlockSpec((1, 1, D_MODEL), lambda i: (_row_group(i + t0, tm), 0, 5)), vec, vec],
        out_specs=pl.BlockSpec((tm, D_MODEL), lambda i: (i, 0)),
        out_shape=jax.ShapeDtypeStruct((n_rows, D_MODEL), F32),
        compiler_params=_cparams(1),
        name=f"residual_ln2_r{row0}",
    )(x1, ffn, mods3, g, b)


def _grid_pos_embed():
    t = np.arange(LAT_T)
    row = (t // GRID_W).astype(np.float32)
    col = (t % GRID_W).astype(np.float32)
    quarter = D_MODEL // 4
    omega = 1.0 / (POS_BASE ** (jnp.arange(quarter, dtype=F32) / quarter))
    er = jnp.asarray(row)[:, None] * omega
    ec = jnp.asarray(col)[:, None] * omega
    return jnp.concatenate([jnp.sin(er), jnp.cos(er), jnp.sin(ec), jnp.cos(ec)], axis=-1)


def _lane_vec(v2x16):
    return jnp.zeros((1, HEAD_D), F32).at[0, 2 * HEADS:4 * HEADS].set(v2x16.reshape(-1))


def kernel(x_prompt, x_sample, state_lru, state_dn, c, c_ctx, w_mod, b_mod, w_in, lru_conv_w, lru_conv_b, lru_gate_w, lru_gate_b, lru_lambda, dn_conv_w, dn_a_log, dn_dt_bias, dn_norm_w, b_branch, w_lru_proj, w_dn_proj, w_o, ln1_g, ln1_b, w_up, w_down, ln2_g, ln2_b):
    xp = x_prompt.reshape(M_CTX, D_MODEL)
    xs = x_sample.reshape(M_LAT, D_MODEL)
    pos = _grid_pos_embed()

    cc = jnp.zeros((8, D_MODEL), F32).at[0].set(c_ctx).at[1:1 + N_LAT_SEQ].set(c)
    mods = _modulation(cc, w_mod[0], b_mod)
    mods3 = mods.reshape(8, 1, 6 * D_MODEL)

    h = _prep(xp, xs, pos, mods3)

    w_in0 = w_in[0]
    p_main = _mm(h, w_in0, col_block=0, n_cols=N_MAIN, tn=512, tm=512, out_dtype=F32, name="in_proj_main")
    p_bd = _mm(h, w_in0, col_block=COL_BD // HEAD_D, n_cols=HEAD_D, tn=HEAD_D, tm=512, out_dtype=F32,
               name="in_proj_beta_decay")
    gates = _mm(h, w_in0[:, COL_GATE:], col_block=0, n_cols=2 * D_MODEL, tn=512, tm=512, out_dtype=F32,
                name="in_proj_gates")

    gate_b4 = lru_gate_b[0].reshape(4, D_LRU)
    lru_args = (lru_conv_w[0], lru_conv_b, lru_gate_w[0], gate_b4, lru_lambda[0])
    lru_ctx, lru_state = _lru(p_main, *lru_args, jnp.zeros((N_CTX_SEQ, 2, D_LRU), F32),
                              t_len=CTX_T, row_block0=0, n_seq=N_CTX_SEQ)
    lru_lat, _ = _lru(p_main, *lru_args, state_lru[:, 0],
                      t_len=LAT_T, row_block0=M_CTX // LAT_T, n_seq=N_LAT_SEQ)

    alog_vec = _lane_vec(dn_a_log[0])
    dt_vec = _lane_vec(dn_dt_bias[0])
    dn_args = (dn_conv_w[0], alog_vec, dt_vec, dn_norm_w)
    dn_ctx, dn_state = _gdn(p_main, p_bd, *dn_args, None,
                            t_len=CTX_T, row_block0=0, n_seq=N_CTX_SEQ, emit_state=True)
    (dn_lat,) = _gdn(p_main, p_bd, *dn_args, state_dn[:, 0],
                     t_len=LAT_T, row_block0=M_CTX // LAT_T, n_seq=N_LAT_SEQ, emit_state=False)

    lru_out = jnp.concatenate([lru_ctx, lru_lat], axis=0)
    dn_out = jnp.concatenate([dn_ctx, dn_lat], axis=0)
    merged = _merge(lru_out, dn_out, w_lru_proj[0], w_dn_proj[0], gates, b_branch[0].reshape(2, 1, D_MODEL))
    mix = _mm(merged, w_o[0], col_block=0, n_cols=D_MODEL, tn=512, tm=512, out_dtype=F32, name="out_proj")

    x1, h2 = _ln1(xp, xs, pos, mix, mods3, ln1_g, ln1_b)
    u = _mm(h2, w_up[0], col_block=0, n_cols=D_FF, tn=512, tm=512, out_dtype=BF16, relu2=True, name="up_proj")
    ffn = _down(u, w_down[0])

    y_ctx = _ln2(x1, ffn, mods3, ln2_g, ln2_b, row0=0, n_rows=M_CTX)
    y_lat = _ln2(x1, ffn, mods3, ln2_g, ln2_b, row0=M_CTX, n_rows=M_LAT)

    return (y_ctx.reshape(x_prompt.shape), y_lat.reshape(x_sample.shape),
            lru_state.reshape(N_CTX_SEQ, 1, 2, D_LRU),
            dn_state.reshape(N_CTX_SEQ, 1, 2, HEADS, HEAD_D, HEAD_D))
```

```python
import functools
import math

import jax
import jax.numpy as jnp
import numpy as np
from jax import lax
from jax.experimental import pallas as pl
from jax.experimental.pallas import tpu as pltpu

F32 = jnp.float32
BF16 = jnp.bfloat16

D_MODEL = 4096
N_CTX_SEQ, CTX_T = 16, 256
N_LAT_SEQ, LAT_T = 2, 1024
M_CTX = N_CTX_SEQ * CTX_T
M_LAT = N_LAT_SEQ * LAT_T
M_TOK = M_CTX + M_LAT
GRID_W = 64
D_LRU = 2048
LRU_C = 8.0
HEADS = 16
HEAD_D = 128
DN_QK = HEADS * HEAD_D
CHUNK = 64
SUPER = 256
DN_HEADS_PER_STEP = 2
D_FF = 4 * D_MODEL
ALPHA = 2.0 ** 0.25
LN_EPS = 1e-5
RMS_EPS = 1e-6
POS_BASE = 10000.0
COL_LRU_X, COL_LRU_Y, COL_Q, COL_K, COL_V, COL_Z = 0, 2048, 4096, 6144, 8192, 10240
COL_BD = 12288
COL_GATE = 12352
N_MAIN = COL_BD
VMEM_LIMIT = 56 * 1024 * 1024
TM = 1024
TN = 512


def _cparams(n_axes):
    return pltpu.CompilerParams(dimension_semantics=("arbitrary",) * n_axes, vmem_limit_bytes=VMEM_LIMIT)


def _row_group(i, tm):
    n_ctx = M_CTX // tm
    return jnp.where(i < n_ctx, 0, 1 + (i - n_ctx) // (LAT_T // tm))


def _mod_spec(tm, k):
    return pl.BlockSpec((1, 1, D_MODEL), lambda i: (_row_group(i, tm), 0, k))


def _sigmoid(x):
    return 1.0 / (1.0 + jnp.exp(-x))


def _silu(x):
    return x * _sigmoid(x)


def _softplus(x):
    return jnp.maximum(x, 0.0) + jnp.log1p(jnp.exp(-jnp.abs(x)))


def _dot(a, b):
    return jnp.dot(a, b, preferred_element_type=F32)


def _mod_kernel(c_ref, w_ref, b_ref, o_ref):
    s = _silu(c_ref[...]).astype(BF16)
    o_ref[...] = _dot(s, w_ref[...].astype(BF16)) + b_ref[...]


def _modulation(cc, w_mod, b_mod):
    tn = 512
    n = w_mod.shape[1]
    return pl.pallas_call(
        _mod_kernel,
        grid=(n // tn,),
        in_specs=[pl.BlockSpec((8, D_MODEL), lambda j: (0, 0)),
                  pl.BlockSpec((D_MODEL, tn), lambda j: (0, j)),
                  pl.BlockSpec((1, tn), lambda j: (0, j))],
        out_specs=pl.BlockSpec((8, tn), lambda j: (0, j)),
        out_shape=jax.ShapeDtypeStruct((8, n), F32),
        compiler_params=_cparams(1),
        name="modulation",
    )(cc, w_mod, b_mod)


def _token_specs(tm):
    n_ctx = M_CTX // tm
    n_pos = LAT_T // tm
    return [pl.BlockSpec((tm, D_MODEL), lambda i: (jnp.minimum(i, n_ctx - 1), 0)),
            pl.BlockSpec((tm, D_MODEL), lambda i: (jnp.maximum(i - n_ctx, 0), 0)),
            pl.BlockSpec((tm, D_MODEL), lambda i: (jnp.maximum(i - n_ctx, 0) % n_pos, 0))]


def _prep_kernel(xp_ref, xs_ref, pos_ref, sh_ref, sc_ref, h_ref, *, n_ctx_tiles):
    i = pl.program_id(0)
    sc = 1.0 + sc_ref[0]
    sh = sh_ref[0]

    @pl.when(i < n_ctx_tiles)
    def _():
        h_ref[...] = (xp_ref[...] * sc + sh).astype(BF16)

    @pl.when(i >= n_ctx_tiles)
    def _():
        h_ref[...] = ((xs_ref[...] + pos_ref[...]) * sc + sh).astype(BF16)


def _prep(xp, xs, pos, mods3):
    tm = 128
    return pl.pallas_call(
        functools.partial(_prep_kernel, n_ctx_tiles=M_CTX // tm),
        grid=(M_TOK // tm,),
        in_specs=_token_specs(tm) + [_mod_spec(tm, 0), _mod_spec(tm, 1)],
        out_specs=pl.BlockSpec((tm, D_MODEL), lambda i: (i, 0)),
        out_shape=jax.ShapeDtypeStruct((M_TOK, D_MODEL), BF16),
        compiler_params=_cparams(1),
        name="prep",
    )(xp, xs, pos, mods3, mods3)


def _mm_kernel(lhs_ref, w_ref, o_ref, wb_ref, *, relu2, w_transposed):
    @pl.when(pl.program_id(1) == 0)
    def _():
        w = w_ref[...]
        wb_ref[...] = (w.T if w_transposed else w).astype(BF16)

    acc = _dot(lhs_ref[...], wb_ref[...])
    if relu2:
        acc = jnp.maximum(acc, 0.0)
        acc = acc * acc
    o_ref[...] = acc.astype(o_ref.dtype)


def _mm(lhs, w, *, col0, n_cols, tn, out_dtype, relu2=False, w_transposed=False, name):
    m, k = lhs.shape
    if w_transposed:
        assert col0 % 8 == 0 and tn % 8 == 0
        w_spec = pl.BlockSpec((pl.Element(tn), pl.Element(k)), lambda j, i: (pl.multiple_of(col0 + j * tn, 8), 0))
    else:
        w_spec = pl.BlockSpec((k, tn), lambda j, i: (0, col0 // tn + j))
    return pl.pallas_call(
        functools.partial(_mm_kernel, relu2=relu2, w_transposed=w_transposed),
        grid=(n_cols // tn, m // TM),
        in_specs=[pl.BlockSpec((TM, k), lambda j, i: (i, 0)), w_spec],
        out_specs=pl.BlockSpec((TM, tn), lambda j, i: (i, j)),
        out_shape=jax.ShapeDtypeStruct((m, n_cols), out_dtype),
        scratch_shapes=[pltpu.VMEM((k, tn), BF16)],
        compiler_params=_cparams(2),
        name=name,
    )(lhs, w)


def _down_kernel(lhs_ref, w_ref, o_ref, wb_ref):
    k = pl.program_id(1)
    i = pl.program_id(2)

    @pl.when(i == 0)
    def _():
        wb_ref[...] = w_ref[...].astype(BF16)

    rows = pl.ds(pl.multiple_of(i * TM, TM), TM)
    acc = _dot(lhs_ref[...], wb_ref[...])

    @pl.when(k == 0)
    def _():
        o_ref[rows, :] = acc

    @pl.when(k > 0)
    def _():
        o_ref[rows, :] += acc


def _down(u, w_down):
    tk = 2048
    m = u.shape[0]
    return pl.pallas_call(
        _down_kernel,
        grid=(D_MODEL // TN, D_FF // tk, m // TM),
        in_specs=[pl.BlockSpec((TM, tk), lambda j, k, i: (i, k)),
                  pl.BlockSpec((tk, TN), lambda j, k, i: (k, j))],
        out_specs=pl.BlockSpec((m, TN), lambda j, k, i: (0, j)),
        out_shape=jax.ShapeDtypeStruct((m, D_MODEL), F32),
        scratch_shapes=[pltpu.VMEM((tk, TN), BF16)],
        compiler_params=_cparams(3),
        name="down_proj",
    )(u, w_down)


def _conv4(x, w):
    t = x.shape[0]
    row = lax.broadcasted_iota(jnp.int32, x.shape, 0)
    xm2 = jnp.where(row >= 2, pltpu.roll(x, 2, 0), 0.0)
    xm1 = jnp.where(row >= 1, pltpu.roll(x, 1, 0), 0.0)
    xp1 = jnp.where(row < t - 1, pltpu.roll(x, t - 1, 0), 0.0)
    return w[0:1] * xm2 + w[1:2] * xm1 + w[2:3] * x + w[3:4] * xp1


def _lru_kernel(x_ref, y_ref, cw_ref, cb_ref, gw_ref, gb_ref, lam_ref, h0_ref, o_ref, st_ref,
                af_ref, bf_ref, ab_ref, bb_ref, *, t_len, width):
    for n in range(width // HEAD_D):
        sl = slice(n * HEAD_D, (n + 1) * HEAD_D)
        xc = _conv4(x_ref[:, sl], cw_ref[:, sl]) + cb_ref[:, sl]
        xcb = xc.astype(BF16)
        for d, (a_ref, b_ref) in enumerate(((af_ref, bf_ref), (ab_ref, bb_ref))):
            pre_r = _dot(xcb, gw_ref[d, 0, n].astype(BF16))
            pre_i = _dot(xcb, gw_ref[d, 1, n].astype(BF16))
            r = _sigmoid(pre_r + gb_ref[2 * d:2 * d + 1, sl])
            ig = _sigmoid(pre_i + gb_ref[2 * d + 1:2 * d + 2, sl])
            log_a = (-LRU_C) * r * _softplus(-lam_ref[d:d + 1, sl])
            a = jnp.exp(log_a)
            mult = jnp.sqrt((1.0 + a * a) * jnp.tanh(-log_a))
            a_ref[:, sl] = a
            b_ref[:, sl] = mult * (ig * xc)

    n_tiles = t_len // 8
    rowi = lax.broadcasted_iota(jnp.int32, (8, width), 0)

    def body(g, carry):
        cf, cb = carry
        r0 = pl.multiple_of(g * 8, 8)
        a8 = af_ref[pl.ds(r0, 8), :]
        b8 = bf_ref[pl.ds(r0, 8), :]
        for dd in (1, 2, 4):
            m = rowi >= dd
            a_sh = jnp.where(m, pltpu.roll(a8, dd, 0), 1.0)
            b_sh = jnp.where(m, pltpu.roll(b8, dd, 0), 0.0)
            b8 = a8 * b_sh + b8
            a8 = a8 * a_sh
        h8 = a8 * cf + b8
        bf_ref[pl.ds(r0, 8), :] = h8
        cf = h8[7:8, :]

        r1 = pl.multiple_of((n_tiles - 1 - g) * 8, 8)
        a8 = ab_ref[pl.ds(r1, 8), :]
        b8 = bb_ref[pl.ds(r1, 8), :]
        for dd in (1, 2, 4):
            m = rowi < 8 - dd
            a_sh = jnp.where(m, pltpu.roll(a8, 8 - dd, 0), 1.0)
            b_sh = jnp.where(m, pltpu.roll(b8, 8 - dd, 0), 0.0)
            b8 = a8 * b_sh + b8
            a8 = a8 * a_sh
        h8 = a8 * cb + b8
        bb_ref[pl.ds(r1, 8), :] = h8
        cb = h8[0:1, :]
        return cf, cb

    cf, cb = lax.fori_loop(0, n_tiles, body, (h0_ref[0, 0:1, :], h0_ref[0, 1:2, :]))
    st_ref[0, 0:1, :] = cf
    st_ref[0, 1:2, :] = cb
    y = y_ref[...]
    gelu = 0.5 * y * (1.0 + jnp.tanh(math.sqrt(2.0 / math.pi) * (y + 0.044715 * (y * y * y))))
    o_ref[...] = ((bf_ref[...] + bb_ref[...]) * gelu).astype(BF16)


def _lru(p_main, conv_w, conv_b, gate_w, gate_b4, lam, h0, *, t_len, row_block0, n_seq):
    width = 256
    nblk = width // HEAD_D
    ncb = D_LRU // width
    return pl.pallas_call(
        functools.partial(_lru_kernel, t_len=t_len, width=width),
        grid=(n_seq, ncb),
        in_specs=[pl.BlockSpec((t_len, width), lambda b, c: (b + row_block0, c)),
                  pl.BlockSpec((t_len, width), lambda b, c: (b + row_block0, c + COL_LRU_Y // width)),
                  pl.BlockSpec((4, width), lambda b, c: (0, c)),
                  pl.BlockSpec((1, width), lambda b, c: (0, c)),
                  pl.BlockSpec((2, 2, nblk, HEAD_D, HEAD_D), lambda b, c: (0, 0, c, 0, 0)),
                  pl.BlockSpec((4, width), lambda b, c: (0, c)),
                  pl.BlockSpec((2, width), lambda b, c: (0, c)),
                  pl.BlockSpec((1, 2, width), lambda b, c: (b, 0, c))],
        out_specs=[pl.BlockSpec((t_len, width), lambda b, c: (b, c)),
                   pl.BlockSpec((1, 2, width), lambda b, c: (b, 0, c))],
        out_shape=[jax.ShapeDtypeStruct((n_seq * t_len, D_LRU), BF16),
                   jax.ShapeDtypeStruct((n_seq, 2, D_LRU), F32)],
        scratch_shapes=[pltpu.VMEM((t_len, width), F32)] * 4,
        compiler_params=_cparams(2),
        name=f"rg_lru_t{t_len}",
    )(p_main, p_main, conv_w, conv_b, gate_w, gate_b4, lam, h0)


N_LEVELS = 6
MASK_INCL, MASK_STRICT, MASK_LEVEL0, MASK_EYE = 0, 1, 2, 2 + N_LEVELS


def _dn_masks():
    i = np.arange(SUPER)[:, None]
    j = np.arange(SUPER)[None, :]
    same = (i // CHUNK) == (j // CHUNK)
    out = np.zeros((2, MASK_EYE + 1, SUPER, SUPER), np.float32)
    for d in range(2):
        before = (j > i) if d == 1 else (j < i)
        out[d, MASK_INCL] = same & (before | (i == j))
        out[d, MASK_STRICT] = same & before
        s = 1
        for lv in range(N_LEVELS):
            blk = (i // (2 * s)) == (j // (2 * s))
            il, jl = i % (2 * s), j % (2 * s)
            out[d, MASK_LEVEL0 + lv] = blk & (((il < s) & (jl >= s)) if d == 1 else ((il >= s) & (jl < s)))
            s *= 2
        out[d, MASK_EYE] = (i == j)
    return out


def _split2(x):
    hi = x.astype(BF16)
    return hi, (x - hi.astype(F32)).astype(BF16)


def _dot3(a, b):
    a_hi, a_lo = _split2(a)
    b_hi, b_lo = _split2(b)
    return _dot(a_hi, b_hi) + (_dot(a_hi, b_lo) + _dot(a_lo, b_hi))


def _dn_block(chains, m_ref, mb_ref, s_ref, vn_ref):
    nt = (((1,), (1,)), ((), ()))
    tn = (((0,), (0,)), ((), ()))
    n_chunks = SUPER // CHUNK
    st = []
    for q, k, v, beta_b, g_b, d, idx in chains:
        incl_b = mb_ref[d]
        g_hi = g_b.astype(BF16)
        r = g_b - g_hi.astype(F32)
        g_mid = r.astype(BF16)
        g_lo = (r - g_mid.astype(F32)).astype(BF16)
        gam1 = _dot(incl_b, g_hi) + (_dot(incl_b, g_mid) + _dot(incl_b, g_lo))
        kb = k * beta_b
        kbf = k.astype(BF16)
        kk = lax.dot_general(kb.astype(BF16), kbf, nt, preferred_element_type=F32)
        qk = lax.dot_general(q.astype(BF16), kbf, nt, preferred_element_type=F32)
        st.append(dict(q=q, k=k, v=v, beta_b=beta_b, d=d, idx=idx, gam1=gam1, kb=kb, kk=kk, qk=qk))
    for c in st:
        d = c["d"]
        gam = jnp.concatenate([c["gam1"], c["gam1"]], axis=1)
        decay = jnp.exp(jnp.minimum(gam - gam.T, 0.0)) * m_ref[d, MASK_INCL]
        c["lmat"] = c.pop("kk") * decay * m_ref[d, MASK_STRICT]
        c["attn"] = (c.pop("qk") * decay).astype(BF16)
        c["minv"] = m_ref[d, MASK_EYE] - c["lmat"] * m_ref[d, MASK_LEVEL0]

    for lv in range(1, N_LEVELS):
        for c in st:
            c["mb"] = c["minv"].astype(BF16)
            c["x"] = _dot((c["lmat"] * m_ref[c["d"], MASK_LEVEL0 + lv]).astype(BF16), c["mb"]).astype(BF16)
        for c in st:
            c["minv"] = c["minv"] - _dot(c.pop("mb"), c.pop("x"))
    for c in st:
        c["resid"] = (m_ref[c["d"], MASK_EYE] - c["minv"] - _dot3(c["lmat"], c["minv"])).astype(BF16)
    for c in st:
        c["minv"] = c["minv"] + _dot(c["minv"].astype(BF16), c.pop("resid"))
    for c in st:
        eg = jnp.exp(c["gam1"])
        rhs = jnp.concatenate([c["v"] * c["beta_b"], c["kb"] * eg], axis=1)
        uw = _dot3(c.pop("minv"), rhs)
        c["u"], c["w"] = uw[:, :HEAD_D], uw[:, HEAD_D:].astype(BF16)
        c["q_dec"] = (c["q"] * eg).astype(BF16)
        lasts = []
        for n in range(n_chunks):
            r = n * CHUNK if c["d"] == 1 else n * CHUNK + CHUNK - 1
            lasts.append(jnp.broadcast_to(c["gam1"][r:r + 1, :], (CHUNK, HEAD_D)))
        c["g_last"] = jnp.concatenate(lasts, axis=0)
        c["k_dec"] = (c["k"] * jnp.exp(c["g_last"] - c["gam1"])).astype(BF16)
        c["outs"] = [None] * n_chunks
        vn_ref[c["idx"]] = jnp.zeros((SUPER, HEAD_D), F32)

    for step in range(n_chunks):
        for c in st:
            n = n_chunks - 1 - step if c["d"] == 1 else step
            rows = slice(n * CHUNK, (n + 1) * CHUNK)
            c["s"] = s_ref[c["idx"]]
            c["sb"] = c["s"].astype(BF16)
            c["v_new"] = c["u"][rows] - _dot(c["w"][rows], c["sb"])
            vn_ref[c["idx"], rows, :] = c["v_new"]
        for c in st:
            n = n_chunks - 1 - step if c["d"] == 1 else step
            rows = slice(n * CHUNK, (n + 1) * CHUNK)
            c["outs"][n] = (_dot(c["q_dec"][rows], c["sb"])
                            + _dot(c["attn"][rows], vn_ref[c["idx"]].astype(BF16)))
            inc = lax.dot_general(c["k_dec"][rows], c["v_new"].astype(BF16), tn, preferred_element_type=F32)
            s_ref[c["idx"]] = c["s"] * jnp.exp(c["g_last"][n * CHUNK:n * CHUNK + 1, :]) + inc
    return [jnp.concatenate(c["outs"], axis=0) for c in st]


def _gdn_kernel(*refs, t_len, use_state, emit_state):
    hps = DN_HEADS_PER_STEP
    (q_ref, k_ref, v_ref, z_ref, bd_ref, cwq_ref, cwk_ref, cwv_ref, alog_ref, dt_ref, nw_ref,
     m_ref, mb_ref), refs = refs[:13], refs[13:]
    if use_state:
        s0_ref, refs = refs[0], refs[1:]
    o_ref, refs = refs[0], refs[1:]
    if emit_state:
        st_ref, refs = refs[0], refs[1:]
    qs_ref, ks_ref, vs_ref, bet_ref, g_ref, acc_ref, s_ref, vn_ref = refs

    head0 = pl.program_id(1) * hps

    def l2n(x):
        return x * lax.rsqrt(jnp.sum(x * x, axis=-1, keepdims=True) + RMS_EPS)

    bd = bd_ref[...]
    beta_all = _sigmoid(bd)
    g_all = -jnp.exp(alog_ref[...]) * _softplus(bd + dt_ref[...])
    lane = lax.broadcasted_iota(jnp.int32, bd.shape, 1)
    for hh in range(hps):
        sl = slice(hh * HEAD_D, (hh + 1) * HEAD_D)
        qs_ref[:, sl] = l2n(_silu(_conv4(q_ref[:, sl], cwq_ref[:, sl]))) * (HEAD_D ** -0.5)
        ks_ref[:, sl] = l2n(_silu(_conv4(k_ref[:, sl], cwk_ref[:, sl])))
        vs_ref[:, sl] = _silu(_conv4(v_ref[:, sl], cwv_ref[:, sl]))
        for d in range(2):
            col = d * HEADS + head0 + hh
            beta = jnp.sum(jnp.where(lane == col, beta_all, 0.0), axis=1, keepdims=True)
            g = jnp.sum(jnp.where(lane == 2 * HEADS + col, g_all, 0.0), axis=1, keepdims=True)
            bet_ref[d, :, sl] = jnp.broadcast_to(beta, (t_len, HEAD_D))
            g_ref[d, :, sl] = jnp.broadcast_to(g, (t_len, HEAD_D))
            if use_state:
                s_ref[2 * hh + d] = s0_ref[0, d, hh]
            else:
                s_ref[2 * hh + d] = jnp.zeros((HEAD_D, HEAD_D), F32)

    n_blocks = t_len // SUPER

    def block_step(nb):
        chains, dests = [], []
        for hh in range(hps):
            sl = slice(hh * HEAD_D, (hh + 1) * HEAD_D)
            for d in range(2):
                blk = nb if d == 0 else n_blocks - 1 - nb
                rows = pl.ds(pl.multiple_of(blk * SUPER, SUPER), SUPER)
                chains.append((qs_ref[rows, sl], ks_ref[rows, sl], vs_ref[rows, sl],
                               bet_ref[d, rows, sl], g_ref[d, rows, sl], d, 2 * hh + d))
                dests.append((d, rows, sl))
        for (d, rows, sl), out in zip(dests, _dn_block(chains, m_ref, mb_ref, s_ref, vn_ref)):
            acc_ref[d, rows, sl] = out

    if n_blocks == 1:
        block_step(0)
    else:
        def loop_body(nb, carry):
            block_step(nb)
            return carry
        lax.fori_loop(0, n_blocks, loop_body, 0)

    for hh in range(hps):
        sl = slice(hh * HEAD_D, (hh + 1) * HEAD_D)
        if emit_state:
            for d in range(2):
                st_ref[0, d, hh] = s_ref[2 * hh + d]
        o = acc_ref[0, :, sl] + acc_ref[1, :, sl]
        o = o * lax.rsqrt(jnp.mean(o * o, axis=-1, keepdims=True) + RMS_EPS) * nw_ref[...]
        o_ref[:, sl] = (o * _silu(z_ref[:, sl])).astype(BF16)


def _gdn(p_main, p_bd, conv_w, alog_vec, dt_vec, norm_w, masks, masks_incl_b, s0, *,
         t_len, row_block0, n_seq, emit_state):
    use_state = s0 is not None
    hps = DN_HEADS_PER_STEP
    wid = hps * HEAD_D
    cb = lambda col: col // wid
    tok = lambda col: pl.BlockSpec((t_len, wid), lambda b, h: (b + row_block0, h + cb(col)))
    cws = lambda col: pl.BlockSpec((4, wid), lambda b, h: (0, h + cb(col - COL_Q)))
    vec = pl.BlockSpec((1, HEAD_D), lambda b, h: (0, 0))
    state_spec = pl.BlockSpec((1, 2, hps, HEAD_D, HEAD_D), lambda b, h: (b, 0, h, 0, 0))
    in_specs = [tok(COL_Q), tok(COL_K), tok(COL_V), tok(COL_Z),
                pl.BlockSpec((t_len, HEAD_D), lambda b, h: (b + row_block0, 0)),
                cws(COL_Q), cws(COL_K), cws(COL_V), vec, vec, vec,
                pl.BlockSpec(masks.shape, lambda b, h: (0, 0, 0, 0)),
                pl.BlockSpec(masks_incl_b.shape, lambda b, h: (0, 0, 0))]
    args = [p_main, p_main, p_main, p_main, p_bd, conv_w, conv_w, conv_w, alog_vec, dt_vec, norm_w,
            masks, masks_incl_b]
    if use_state:
        in_specs.append(state_spec)
        args.append(s0)
    out_specs = [pl.BlockSpec((t_len, wid), lambda b, h: (b, h))]
    out_shape = [jax.ShapeDtypeStruct((n_seq * t_len, DN_QK), BF16)]
    if emit_state:
        out_specs.append(state_spec)
        out_shape.append(jax.ShapeDtypeStruct((n_seq, 2, HEADS, HEAD_D, HEAD_D), F32))
    scratch = ([pltpu.VMEM((t_len, wid), F32)] * 3
               + [pltpu.VMEM((2, t_len, wid), F32)] * 3
               + [pltpu.VMEM((2 * hps, HEAD_D, HEAD_D), F32),
                  pltpu.VMEM((2 * hps, SUPER, HEAD_D), F32)])
    return pl.pallas_call(
        functools.partial(_gdn_kernel, t_len=t_len, use_state=use_state, emit_state=emit_state),
        grid=(n_seq, HEADS // hps),
        in_specs=in_specs, out_specs=out_specs, out_shape=out_shape,
        scratch_shapes=scratch,
        compiler_params=_cparams(2),
        name=f"gated_delta_t{t_len}",
    )(*args)


def _merge_kernel(ac_ref, al_ref, bc_ref, bl_ref, wl_ref, wd_ref, g0_ref, g1_ref, b0_ref, b1_ref, o_ref,
                  wlb_ref, wdb_ref, *, n_ctx_tiles):
    i = pl.program_id(1)

    @pl.when(i == 0)
    def _():
        wlb_ref[...] = wl_ref[...].astype(BF16)
        wdb_ref[...] = wd_ref[...].astype(BF16)

    def finish(a, b):
        pl_ = _dot(a, wlb_ref[...])
        pd_ = _dot(b, wdb_ref[...])
        o_ref[...] = (_sigmoid(g0_ref[...] + b0_ref[0]) * pl_ + _sigmoid(g1_ref[...] + b1_ref[0]) * pd_).astype(BF16)

    @pl.when(i < n_ctx_tiles)
    def _():
        finish(ac_ref[...], bc_ref[...])

    @pl.when(i >= n_ctx_tiles)
    def _():
        finish(al_ref[...], bl_ref[...])


def _merge(lru_ctx, lru_lat, dn_ctx, dn_lat, w_lru, w_dn, gates, b_branch3):
    tm = 512
    nj = D_MODEL // TN
    n_ctx = M_CTX // tm
    ctx = pl.BlockSpec((tm, D_LRU), lambda j, i: (jnp.minimum(i, n_ctx - 1), 0))
    lat = pl.BlockSpec((tm, D_LRU), lambda j, i: (jnp.maximum(i - n_ctx, 0), 0))
    return pl.pallas_call(
        functools.partial(_merge_kernel, n_ctx_tiles=n_ctx),
        grid=(nj, M_TOK // tm),
        in_specs=[ctx, lat, ctx, lat,
                  pl.BlockSpec((D_LRU, TN), lambda j, i: (0, j)),
                  pl.BlockSpec((DN_QK, TN), lambda j, i: (0, j)),
                  pl.BlockSpec((tm, TN), lambda j, i: (i, j)),
                  pl.BlockSpec((tm, TN), lambda j, i: (i, j + nj)),
                  pl.BlockSpec((1, 1, TN), lambda j, i: (0, 0, j)),
                  pl.BlockSpec((1, 1, TN), lambda j, i: (1, 0, j))],
        out_specs=pl.BlockSpec((tm, TN), lambda j, i: (i, j)),
        out_shape=jax.ShapeDtypeStruct((M_TOK, D_MODEL), BF16),
        scratch_shapes=[pltpu.VMEM((D_LRU, TN), BF16), pltpu.VMEM((DN_QK, TN), BF16)],
        compiler_params=_cparams(2),
        name="branch_merge",
    )(lru_ctx, lru_lat, dn_ctx, dn_lat, w_lru, w_dn, gates, gates, b_branch3, b_branch3)


def _layer_norm(x, g, b):
    mu = jnp.mean(x, axis=-1, keepdims=True)
    xc = x - mu
    var = jnp.mean(xc * xc, axis=-1, keepdims=True)
    return xc * lax.rsqrt(var + LN_EPS) * g + b


def _ln1_kernel(xp_ref, xs_ref, pos_ref, mix_ref, gm_ref, shf_ref, scf_ref, g_ref, b_ref, x1_ref, h2_ref,
                *, n_ctx_tiles):
    i = pl.program_id(0)

    def finish(x):
        x1 = _layer_norm(ALPHA * x + gm_ref[0] * mix_ref[...], g_ref[...], b_ref[...])
        x1_ref[...] = x1
        h2_ref[...] = (x1 * (1.0 + scf_ref[0]) + shf_ref[0]).astype(BF16)

    @pl.when(i < n_ctx_tiles)
    def _():
        finish(xp_ref[...])

    @pl.when(i >= n_ctx_tiles)
    def _():
        finish(xs_ref[...] + pos_ref[...])


def _ln1(xp, xs, pos, mix, mods3, g, b):
    tm = 128
    row = pl.BlockSpec((tm, D_MODEL), lambda i: (i, 0))
    vec = pl.BlockSpec((1, D_MODEL), lambda i: (0, 0))
    return pl.pallas_call(
        functools.partial(_ln1_kernel, n_ctx_tiles=M_CTX // tm),
        grid=(M_TOK // tm,),
        in_specs=_token_specs(tm) + [row, _mod_spec(tm, 2), _mod_spec(tm, 3), _mod_spec(tm, 4), vec, vec],
        out_specs=[row, row],
        out_shape=[jax.ShapeDtypeStruct((M_TOK, D_MODEL), F32), jax.ShapeDtypeStruct((M_TOK, D_MODEL), BF16)],
        compiler_params=_cparams(1),
        name="residual_ln1",
    )(xp, xs, pos, mix, mods3, mods3, mods3, g, b)


def _ln2_kernel(x1_ref, ffn_ref, gf_ref, g_ref, b_ref, y_ref):
    y_ref[...] = _layer_norm(ALPHA * x1_ref[...] + gf_ref[0] * ffn_ref[...], g_ref[...], b_ref[...])


def _ln2(x1, ffn, mods3, g, b, *, row0, n_rows):
    tm = 128
    t0 = row0 // tm
    row = pl.BlockSpec((tm, D_MODEL), lambda i: (i + t0, 0))
    vec = pl.BlockSpec((1, D_MODEL), lambda i: (0, 0))
    return pl.pallas_call(
        _ln2_kernel,
        grid=(n_rows // tm,),
        in_specs=[row, row, pl.BlockSpec((1, 1, D_MODEL), lambda i: (_row_group(i + t0, tm), 0, 5)), vec, vec],
        out_specs=pl.BlockSpec((tm, D_MODEL), lambda i: (i, 0)),
        out_shape=jax.ShapeDtypeStruct((n_rows, D_MODEL), F32),
        compiler_params=_cparams(1),
        name=f"residual_ln2_r{row0}",
    )(x1, ffn, mods3, g, b)


def _grid_pos_embed():
    t = np.arange(LAT_T)
    row = (t // GRID_W).astype(np.float32)
    col = (t % GRID_W).astype(np.float32)
    quarter = D_MODEL // 4
    omega = 1.0 / (POS_BASE ** (jnp.arange(quarter, dtype=F32) / quarter))
    er = jnp.asarray(row)[:, None] * omega
    ec = jnp.asarray(col)[:, None] * omega
    return jnp.concatenate([jnp.sin(er), jnp.cos(er), jnp.sin(ec), jnp.cos(ec)], axis=-1)


def _lane_vec(v2x16):
    return jnp.zeros((1, HEAD_D), F32).at[0, 2 * HEADS:4 * HEADS].set(v2x16.reshape(-1))


def kernel(x_prompt, x_sample, state_lru, state_dn, c, c_ctx, w_mod, b_mod, w_in, lru_conv_w, lru_conv_b, lru_gate_w, lru_gate_b, lru_lambda, dn_conv_w, dn_a_log, dn_dt_bias, dn_norm_w, b_branch, w_lru_proj, w_dn_proj, w_o, ln1_g, ln1_b, w_up, w_down, ln2_g, ln2_b):
    xp = x_prompt.reshape(M_CTX, D_MODEL)
    xs = x_sample.reshape(M_LAT, D_MODEL)
    pos = _grid_pos_embed()

    cc = jnp.zeros((8, D_MODEL), F32).at[0].set(c_ctx).at[1:1 + N_LAT_SEQ].set(c)
    mods = _modulation(cc, w_mod[0], b_mod)
    mods3 = mods.reshape(8, 1, 6 * D_MODEL)

    h = _prep(xp, xs, pos, mods3)

    w_in_t = jnp.swapaxes(w_in, 1, 2)[0]
    p_main = _mm(h, w_in_t, col0=0, n_cols=N_MAIN, tn=TN, out_dtype=F32, w_transposed=True, name="in_proj_main")
    p_bd = _mm(h, w_in_t, col0=COL_BD, n_cols=HEAD_D, tn=HEAD_D, out_dtype=F32, w_transposed=True,
               name="in_proj_beta_decay")
    gates = _mm(h, w_in_t, col0=COL_GATE, n_cols=2 * D_MODEL, tn=TN, out_dtype=F32, w_transposed=True,
                name="in_proj_gates")

    gate_b4 = lru_gate_b[0].reshape(4, D_LRU)
    lru_args = (lru_conv_w[0], lru_conv_b, lru_gate_w[0], gate_b4, lru_lambda[0])
    lru_ctx, lru_state = _lru(p_main, *lru_args, jnp.zeros((N_CTX_SEQ, 2, D_LRU), F32),
                              t_len=CTX_T, row_block0=0, n_seq=N_CTX_SEQ)
    lru_lat, _ = _lru(p_main, *lru_args, state_lru[:, 0],
                      t_len=LAT_T, row_block0=M_CTX // LAT_T, n_seq=N_LAT_SEQ)

    masks = _dn_masks()
    dn_args = (dn_conv_w[0], _lane_vec(dn_a_log[0]), _lane_vec(dn_dt_bias[0]), dn_norm_w,
               jnp.asarray(masks), jnp.asarray(masks[:, MASK_INCL], dtype=BF16))
    dn_ctx, dn_state = _gdn(p_main, p_bd, *dn_args, None,
                            t_len=CTX_T, row_block0=0, n_seq=N_CTX_SEQ, emit_state=True)
    (dn_lat,) = _gdn(p_main, p_bd, *dn_args, state_dn[:, 0],
                     t_len=LAT_T, row_block0=M_CTX // LAT_T, n_seq=N_LAT_SEQ, emit_state=False)

    merged = _merge(lru_ctx, lru_lat, dn_ctx, dn_lat, w_lru_proj[0], w_dn_proj[0], gates,
                    b_branch[0].reshape(2, 1, D_MODEL))
    mix = _mm(merged, w_o[0], col0=0, n_cols=D_MODEL, tn=TN, out_dtype=F32, name="out_proj")

    x1, h2 = _ln1(xp, xs, pos, mix, mods3, ln1_g, ln1_b)
    u = _mm(h2, w_up[0], col0=0, n_cols=D_FF, tn=TN, out_dtype=BF16, relu2=True, name="up_proj")
    ffn = _down(u, w_down[0])

    y_ctx = _ln2(x1, ffn, mods3, ln2_g, ln2_b, row0=0, n_rows=M_CTX)
    y_lat = _ln2(x1, ffn, mods3, ln2_g, ln2_b, row0=M_CTX, n_rows=M_LAT)

    return (y_ctx.reshape(x_prompt.shape), y_lat.reshape(x_sample.shape),
            lru_state.reshape(N_CTX_SEQ, 1, 2, D_LRU),
            dn_state.reshape(N_CTX_SEQ, 1, 2, HEADS, HEAD_D, HEAD_D))
```

```python
import functools
import math

import jax
import jax.numpy as jnp
import numpy as np
from jax import lax
from jax.experimental import pallas as pl
from jax.experimental.pallas import tpu as pltpu

F32 = jnp.float32
BF16 = jnp.bfloat16

D_MODEL = 4096
N_CTX_SEQ, CTX_T = 16, 256
N_LAT_SEQ, LAT_T = 2, 1024
M_CTX = N_CTX_SEQ * CTX_T
M_LAT = N_LAT_SEQ * LAT_T
M_TOK = M_CTX + M_LAT
GRID_W = 64
D_LRU = 2048
LRU_C = 8.0
HEADS = 16
HEAD_D = 128
DN_QK = HEADS * HEAD_D
CHUNK = 64
SUPER = 256
DN_HEADS_PER_STEP_CTX = 4
DN_HEADS_PER_STEP_LAT = 2
D_FF = 4 * D_MODEL
ALPHA = 2.0 ** 0.25
LN_EPS = 1e-5
RMS_EPS = 1e-6
POS_BASE = 10000.0
COL_LRU_X, COL_LRU_Y, COL_Q, COL_K, COL_V, COL_Z = 0, 2048, 4096, 6144, 8192, 10240
COL_BD = 12288
COL_GATE = 12352
N_MAIN = COL_BD
VMEM_LIMIT = 56 * 1024 * 1024
TM = 1536
TN = 512


def _cparams(n_axes):
    return pltpu.CompilerParams(dimension_semantics=("arbitrary",) * n_axes, vmem_limit_bytes=VMEM_LIMIT)


def _row_group(i, tm):
    n_ctx = M_CTX // tm
    return jnp.where(i < n_ctx, 0, 1 + (i - n_ctx) // (LAT_T // tm))


def _mod_spec(tm, k):
    return pl.BlockSpec((1, 1, D_MODEL), lambda i: (_row_group(i, tm), 0, k))


def _sigmoid(x):
    return 0.5 * (1.0 + jnp.tanh(0.5 * x))


def _silu(x):
    return x * _sigmoid(x)


def _softplus(x):
    return jnp.maximum(x, 0.0) + jnp.log1p(jnp.exp(-jnp.abs(x)))


def _dot(a, b):
    return jnp.dot(a, b, preferred_element_type=F32)


def _mod_kernel(c_ref, w_ref, b_ref, o_ref):
    s = _silu(c_ref[...]).astype(BF16)
    o_ref[...] = _dot(s, w_ref[...].astype(BF16)) + b_ref[...]


def _modulation(cc, w_mod, b_mod):
    tn = 512
    n = w_mod.shape[1]
    return pl.pallas_call(
        _mod_kernel,
        grid=(n // tn,),
        in_specs=[pl.BlockSpec((8, D_MODEL), lambda j: (0, 0)),
                  pl.BlockSpec((D_MODEL, tn), lambda j: (0, j)),
                  pl.BlockSpec((1, tn), lambda j: (0, j))],
        out_specs=pl.BlockSpec((8, tn), lambda j: (0, j)),
        out_shape=jax.ShapeDtypeStruct((8, n), F32),
        compiler_params=_cparams(1),
        name="modulation",
    )(cc, w_mod, b_mod)


def _token_specs(tm):
    n_ctx = M_CTX // tm
    n_pos = LAT_T // tm
    return [pl.BlockSpec((tm, D_MODEL), lambda i: (jnp.minimum(i, n_ctx - 1), 0)),
            pl.BlockSpec((tm, D_MODEL), lambda i: (jnp.maximum(i - n_ctx, 0), 0)),
            pl.BlockSpec((tm, D_MODEL), lambda i: (jnp.maximum(i - n_ctx, 0) % n_pos, 0))]


def _prep_kernel(xp_ref, xs_ref, pos_ref, sh_ref, sc_ref, h_ref, *, n_ctx_tiles):
    i = pl.program_id(0)
    sc = 1.0 + sc_ref[0]
    sh = sh_ref[0]

    @pl.when(i < n_ctx_tiles)
    def _():
        h_ref[...] = (xp_ref[...] * sc + sh).astype(BF16)

    @pl.when(i >= n_ctx_tiles)
    def _():
        h_ref[...] = ((xs_ref[...] + pos_ref[...]) * sc + sh).astype(BF16)


def _prep(xp, xs, pos, mods3):
    tm = 128
    return pl.pallas_call(
        functools.partial(_prep_kernel, n_ctx_tiles=M_CTX // tm),
        grid=(M_TOK // tm,),
        in_specs=_token_specs(tm) + [_mod_spec(tm, 0), _mod_spec(tm, 1)],
        out_specs=pl.BlockSpec((tm, D_MODEL), lambda i: (i, 0)),
        out_shape=jax.ShapeDtypeStruct((M_TOK, D_MODEL), BF16),
        compiler_params=_cparams(1),
        name="prep",
    )(xp, xs, pos, mods3, mods3)


def _mm_kernel(lhs_ref, w_ref, *rest, relu2, sigmoid_bias, w_transposed, kc):
    if sigmoid_bias:
        bias_ref, o_ref, wb_ref = rest
    else:
        o_ref, wb_ref = rest
    k_dim = lhs_ref.shape[1]

    def finish(acc):
        if relu2:
            acc = jnp.maximum(acc, 0.0)
            acc = acc * acc
        if sigmoid_bias:
            acc = _sigmoid(acc + bias_ref[...])
        o_ref[...] = acc.astype(o_ref.dtype)

    @pl.when(pl.program_id(1) == 0)
    def _():
        acc = None
        for c in range(k_dim // kc):
            ks = slice(c * kc, (c + 1) * kc)
            wb = (w_ref[:, ks].T if w_transposed else w_ref[ks, :]).astype(BF16)
            wb_ref[ks, :] = wb
            part = _dot(lhs_ref[:, ks], wb)
            acc = part if acc is None else acc + part
        finish(acc)

    @pl.when(pl.program_id(1) > 0)
    def _():
        finish(_dot(lhs_ref[...], wb_ref[...]))


def _mm(lhs, w, *, col0, n_cols, tn, out_dtype, relu2=False, sigmoid_bias=None, w_transposed=False, name):
    m, k = lhs.shape
    if w_transposed:
        assert col0 % 8 == 0 and tn % 8 == 0
        w_spec = pl.BlockSpec((pl.Element(tn), pl.Element(k)), lambda j, i: (pl.multiple_of(col0 + j * tn, 8), 0))
    else:
        w_spec = pl.BlockSpec((k, tn), lambda j, i: (0, col0 // tn + j))
    in_specs = [pl.BlockSpec((TM, k), lambda j, i: (i, 0)), w_spec]
    args = [lhs, w]
    if sigmoid_bias is not None:
        in_specs.append(pl.BlockSpec((1, tn), lambda j, i: (0, j)))
        args.append(sigmoid_bias)
    return pl.pallas_call(
        functools.partial(_mm_kernel, relu2=relu2, sigmoid_bias=sigmoid_bias is not None,
                          w_transposed=w_transposed, kc=512),
        grid=(n_cols // tn, m // TM),
        in_specs=in_specs,
        out_specs=pl.BlockSpec((TM, tn), lambda j, i: (i, j)),
        out_shape=jax.ShapeDtypeStruct((m, n_cols), out_dtype),
        scratch_shapes=[pltpu.VMEM((k, tn), BF16)],
        compiler_params=_cparams(2),
        name=name,
    )(*args)


def _down_kernel(lhs_ref, w_ref, o_ref, wb_ref, *, tm, kc):
    k = pl.program_id(1)
    i = pl.program_id(2)
    rows = pl.ds(pl.multiple_of(i * tm, tm), tm)
    tk = lhs_ref.shape[1]

    def accumulate(acc):
        @pl.when(k == 0)
        def _():
            o_ref[rows, :] = acc

        @pl.when(k > 0)
        def _():
            o_ref[rows, :] += acc

    @pl.when(i == 0)
    def _():
        acc = None
        for c in range(tk // kc):
            ks = slice(c * kc, (c + 1) * kc)
            wb = w_ref[ks, :].astype(BF16)
            wb_ref[ks, :] = wb
            part = _dot(lhs_ref[:, ks], wb)
            acc = part if acc is None else acc + part
        accumulate(acc)

    @pl.when(i > 0)
    def _():
        accumulate(_dot(lhs_ref[...], wb_ref[...]))


def _down(u, w_down):
    tk, tm = 4096, 1024
    m = u.shape[0]
    return pl.pallas_call(
        functools.partial(_down_kernel, tm=tm, kc=512),
        grid=(D_MODEL // TN, D_FF // tk, m // tm),
        in_specs=[pl.BlockSpec((tm, tk), lambda j, k, i: (i, k)),
                  pl.BlockSpec((tk, TN), lambda j, k, i: (k, j))],
        out_specs=pl.BlockSpec((m, TN), lambda j, k, i: (0, j), pipeline_mode=pl.Buffered(1)),
        out_shape=jax.ShapeDtypeStruct((m, D_MODEL), F32),
        scratch_shapes=[pltpu.VMEM((tk, TN), BF16)],
        compiler_params=_cparams(3),
        name="down_proj",
    )(u, w_down)


def _conv4(x, w):
    t = x.shape[0]
    row = lax.broadcasted_iota(jnp.int32, x.shape, 0)
    xm2 = jnp.where(row >= 2, pltpu.roll(x, 2, 0), 0.0)
    xm1 = jnp.where(row >= 1, pltpu.roll(x, 1, 0), 0.0)
    xp1 = jnp.where(row < t - 1, pltpu.roll(x, t - 1, 0), 0.0)
    return w[0:1] * xm2 + w[1:2] * xm1 + w[2:3] * x + w[3:4] * xp1


def _lru_kernel(x_ref, y_ref, cw_ref, cb_ref, gw_ref, gb_ref, lam_ref, h0_ref, o_ref, st_ref,
                af_ref, bf_ref, ab_ref, bb_ref, *, t_len, width):
    for n in range(width // HEAD_D):
        sl = slice(n * HEAD_D, (n + 1) * HEAD_D)
        xc = _conv4(x_ref[:, sl], cw_ref[:, sl]) + cb_ref[:, sl]
        xcb = xc.astype(BF16)
        for d, (a_ref, b_ref) in enumerate(((af_ref, bf_ref), (ab_ref, bb_ref))):
            pre_r = _dot(xcb, gw_ref[d, 0, n].astype(BF16))
            pre_i = _dot(xcb, gw_ref[d, 1, n].astype(BF16))
            r = _sigmoid(pre_r + gb_ref[2 * d:2 * d + 1, sl])
            ig = _sigmoid(pre_i + gb_ref[2 * d + 1:2 * d + 2, sl])
            log_a = (-LRU_C) * r * _softplus(-lam_ref[d:d + 1, sl])
            a = jnp.exp(log_a)
            mult = jnp.sqrt((1.0 + a * a) * jnp.tanh(-log_a))
            a_ref[:, sl] = a
            b_ref[:, sl] = mult * (ig * xc)

    n_tiles = t_len // 8
    rowi = lax.broadcasted_iota(jnp.int32, (8, width), 0)

    def body(g, carry):
        cf, cb = carry
        r0 = pl.multiple_of(g * 8, 8)
        a8 = af_ref[pl.ds(r0, 8), :]
        b8 = bf_ref[pl.ds(r0, 8), :]
        for dd in (1, 2, 4):
            m = rowi >= dd
            a_sh = jnp.where(m, pltpu.roll(a8, dd, 0), 1.0)
            b_sh = jnp.where(m, pltpu.roll(b8, dd, 0), 0.0)
            b8 = a8 * b_sh + b8
            a8 = a8 * a_sh
        h8 = a8 * cf + b8
        bf_ref[pl.ds(r0, 8), :] = h8
        cf = h8[7:8, :]

        r1 = pl.multiple_of((n_tiles - 1 - g) * 8, 8)
        a8 = ab_ref[pl.ds(r1, 8), :]
        b8 = bb_ref[pl.ds(r1, 8), :]
        for dd in (1, 2, 4):
            m = rowi < 8 - dd
            a_sh = jnp.where(m, pltpu.roll(a8, 8 - dd, 0), 1.0)
            b_sh = jnp.where(m, pltpu.roll(b8, 8 - dd, 0), 0.0)
            b8 = a8 * b_sh + b8
            a8 = a8 * a_sh
        h8 = a8 * cb + b8
        bb_ref[pl.ds(r1, 8), :] = h8
        cb = h8[0:1, :]
        return cf, cb

    cf, cb = lax.fori_loop(0, n_tiles, body, (h0_ref[0, 0:1, :], h0_ref[0, 1:2, :]))
    st_ref[0, 0:1, :] = cf
    st_ref[0, 1:2, :] = cb
    y = y_ref[...]
    gelu = 0.5 * y * (1.0 + jnp.tanh(math.sqrt(2.0 / math.pi) * (y + 0.044715 * (y * y * y))))
    o_ref[...] = ((bf_ref[...] + bb_ref[...]) * gelu).astype(BF16)


def _lru(p_main, conv_w, conv_b, gate_w, gate_b4, lam, h0, *, t_len, row_block0, n_seq):
    width = 256
    nblk = width // HEAD_D
    ncb = D_LRU // width
    return pl.pallas_call(
        functools.partial(_lru_kernel, t_len=t_len, width=width),
        grid=(n_seq, ncb),
        in_specs=[pl.BlockSpec((t_len, width), lambda b, c: (b + row_block0, c)),
                  pl.BlockSpec((t_len, width), lambda b, c: (b + row_block0, c + COL_LRU_Y // width)),
                  pl.BlockSpec((4, width), lambda b, c: (0, c)),
                  pl.BlockSpec((1, width), lambda b, c: (0, c)),
                  pl.BlockSpec((2, 2, nblk, HEAD_D, HEAD_D), lambda b, c: (0, 0, c, 0, 0)),
                  pl.BlockSpec((4, width), lambda b, c: (0, c)),
                  pl.BlockSpec((2, width), lambda b, c: (0, c)),
                  pl.BlockSpec((1, 2, width), lambda b, c: (b, 0, c))],
        out_specs=[pl.BlockSpec((t_len, width), lambda b, c: (b, c)),
                   pl.BlockSpec((1, 2, width), lambda b, c: (b, 0, c))],
        out_shape=[jax.ShapeDtypeStruct((n_seq * t_len, D_LRU), BF16),
                   jax.ShapeDtypeStruct((n_seq, 2, D_LRU), F32)],
        scratch_shapes=[pltpu.VMEM((t_len, width), F32)] * 4,
        compiler_params=_cparams(2),
        name=f"rg_lru_t{t_len}",
    )(p_main, p_main, conv_w, conv_b, gate_w, gate_b4, lam, h0)


N_LEVELS = 6
MASKB_INCL, MASKB_LEVEL0, MASKB_EYE = 0, 1, 1 + N_LEVELS


def _dn_masks():
    i = np.arange(SUPER)[:, None]
    j = np.arange(SUPER)[None, :]
    same = (i // CHUNK) == (j // CHUNK)
    out = np.zeros((2, MASKB_EYE + 1, SUPER, SUPER), np.float32)
    for d in range(2):
        before = (j > i) if d == 1 else (j < i)
        out[d, MASKB_INCL] = same & (before | (i == j))
        s = 1
        for lv in range(N_LEVELS):
            blk = (i // (2 * s)) == (j // (2 * s))
            il, jl = i % (2 * s), j % (2 * s)
            out[d, MASKB_LEVEL0 + lv] = blk & (((il < s) & (jl >= s)) if d == 1 else ((il >= s) & (jl < s)))
            s *= 2
        out[d, MASKB_EYE] = (i == j)
    return out


def _dn_block(chains, g_rows, m_ref, mb_ref, s_ref, vn_ref):
    nt = (((1,), (1,)), ((), ()))
    tn = (((0,), (0,)), ((), ()))
    n_chunks = SUPER // CHUNK
    gam_all = []
    for d in range(2):
        incl_b = mb_ref[d, MASKB_INCL]
        g_hi = g_rows[d].astype(BF16)
        r = g_rows[d] - g_hi.astype(F32)
        g_mid = r.astype(BF16)
        g_lo = (r - g_mid.astype(F32)).astype(BF16)
        gam_all.append(_dot(incl_b, g_hi) + (_dot(incl_b, g_mid) + _dot(incl_b, g_lo)))
    lane = lax.broadcasted_iota(jnp.int32, (SUPER, HEAD_D), 1)
    st = []
    for q, k, v, beta_b, g_lane, d, idx in chains:
        gam_col = jnp.sum(jnp.where(lane == g_lane, gam_all[d], 0.0), axis=1, keepdims=True)
        gam1 = jnp.broadcast_to(gam_col, (SUPER, HEAD_D))
        kb = k * beta_b
        kbf = k.astype(BF16)
        kk = lax.dot_general(kb.astype(BF16), kbf, nt, preferred_element_type=F32)
        qk = lax.dot_general(q.astype(BF16), kbf, nt, preferred_element_type=F32)
        st.append(dict(q=q, k=k, v=v, beta_b=beta_b, d=d, idx=idx, gam1=gam1, gam_col=gam_col, kb=kb, kk=kk, qk=qk))
    for c in st:
        d = c["d"]
        gam = jnp.broadcast_to(c.pop("gam_col"), (SUPER, SUPER))
        decay = jnp.exp(jnp.minimum(gam - gam.T, 0.0)) * m_ref[d]
        c["lmat"] = (c.pop("kk") * decay).astype(BF16)
        c["attn"] = (c.pop("qk") * decay).astype(BF16)
        c["minv"] = mb_ref[d, MASKB_EYE] - c["lmat"] * mb_ref[d, MASKB_LEVEL0]

    for lv in range(1, N_LEVELS):
        for c in st:
            c["x"] = _dot(c["lmat"] * mb_ref[c["d"], MASKB_LEVEL0 + lv], c["minv"]).astype(BF16)
        for c in st:
            c["minv"] = c["minv"] - _dot(c["minv"], c.pop("x")).astype(BF16)
    for c in st:
        eg = jnp.exp(c["gam1"])
        rhs = jnp.concatenate([c["v"] * c["beta_b"], c["kb"] * eg], axis=1)
        uw = _dot(c.pop("minv"), rhs.astype(BF16))
        c["u"], c["w"] = uw[:, :HEAD_D], uw[:, HEAD_D:].astype(BF16)
        c["q_dec"] = (c["q"] * eg).astype(BF16)
        lasts = []
        for n in range(n_chunks):
            r = n * CHUNK if c["d"] == 1 else n * CHUNK + CHUNK - 1
            lasts.append(jnp.broadcast_to(c["gam1"][r:r + 1, :], (CHUNK, HEAD_D)))
        c["g_last"] = jnp.concatenate(lasts, axis=0)
        c["k_dec"] = (c["k"] * jnp.exp(c["g_last"] - c["gam1"])).astype(BF16)
        c["outs"] = [None] * n_chunks
        vn_ref[c["idx"]] = jnp.zeros((SUPER, HEAD_D), F32)

    for step in range(n_chunks):
        for c in st:
            n = n_chunks - 1 - step if c["d"] == 1 else step
            rows = slice(n * CHUNK, (n + 1) * CHUNK)
            c["s"] = s_ref[c["idx"]]
            c["sb"] = c["s"].astype(BF16)
            c["v_new"] = c["u"][rows] - _dot(c["w"][rows], c["sb"])
            vn_ref[c["idx"], rows, :] = c["v_new"]
        for c in st:
            n = n_chunks - 1 - step if c["d"] == 1 else step
            rows = slice(n * CHUNK, (n + 1) * CHUNK)
            c["outs"][n] = (_dot(c["q_dec"][rows], c["sb"])
                            + _dot(c["attn"][rows], vn_ref[c["idx"]].astype(BF16)))
            inc = lax.dot_general(c["k_dec"][rows], c["v_new"].astype(BF16), tn, preferred_element_type=F32)
            s_ref[c["idx"]] = c["s"] * jnp.exp(c["g_last"][n * CHUNK:n * CHUNK + 1, :]) + inc
    return [jnp.concatenate(c["outs"], axis=0) for c in st]


def _gdn_kernel(*refs, t_len, use_state, emit_state, hps):
    (q_ref, k_ref, v_ref, z_ref, bd_ref, cwq_ref, cwk_ref, cwv_ref, alog_ref, dt_ref, nw_ref,
     m_ref, mb_ref), refs = refs[:13], refs[13:]
    if use_state:
        s0_ref, refs = refs[0], refs[1:]
    o_ref, refs = refs[0], refs[1:]
    if emit_state:
        st_ref, refs = refs[0], refs[1:]
    qs_ref, ks_ref, vs_ref, bet_ref, gall_ref, acc_ref, s_ref, vn_ref = refs

    head0 = pl.program_id(1) * hps

    def l2n(x):
        return x * lax.rsqrt(jnp.sum(x * x, axis=-1, keepdims=True) + RMS_EPS)

    bd = bd_ref[...]
    beta_all = _sigmoid(bd)
    gall_ref[...] = -jnp.exp(alog_ref[...]) * _softplus(bd + dt_ref[...])
    lane = lax.broadcasted_iota(jnp.int32, bd.shape, 1)
    for hh in range(hps):
        sl = slice(hh * HEAD_D, (hh + 1) * HEAD_D)
        qs_ref[:, sl] = l2n(_silu(_conv4(q_ref[:, sl], cwq_ref[:, sl]))) * (HEAD_D ** -0.5)
        ks_ref[:, sl] = l2n(_silu(_conv4(k_ref[:, sl], cwk_ref[:, sl])))
        vs_ref[:, sl] = _silu(_conv4(v_ref[:, sl], cwv_ref[:, sl]))
        for d in range(2):
            col = d * HEADS + head0 + hh
            beta = jnp.sum(jnp.where(lane == col, beta_all, 0.0), axis=1, keepdims=True)
            bet_ref[d, :, sl] = jnp.broadcast_to(beta, (t_len, HEAD_D))
            if use_state:
                s_ref[2 * hh + d] = s0_ref[0, d, hh]
            else:
                s_ref[2 * hh + d] = jnp.zeros((HEAD_D, HEAD_D), F32)

    n_blocks = t_len // SUPER

    def block_step(nb):
        chains, dests, g_rows = [], [], []
        for d in range(2):
            blk = nb if d == 0 else n_blocks - 1 - nb
            g_rows.append(gall_ref[pl.ds(pl.multiple_of(blk * SUPER, SUPER), SUPER), :])
        for hh in range(hps):
            sl = slice(hh * HEAD_D, (hh + 1) * HEAD_D)
            for d in range(2):
                blk = nb if d == 0 else n_blocks - 1 - nb
                rows = pl.ds(pl.multiple_of(blk * SUPER, SUPER), SUPER)
                chains.append((qs_ref[rows, sl], ks_ref[rows, sl], vs_ref[rows, sl], bet_ref[d, rows, sl],
                               2 * HEADS + d * HEADS + head0 + hh, d, 2 * hh + d))
                dests.append((d, rows, sl))
        for (d, rows, sl), out in zip(dests, _dn_block(chains, g_rows, m_ref, mb_ref, s_ref, vn_ref)):
            acc_ref[d, rows, sl] = out

    if n_blocks == 1:
        block_step(0)
    else:
        def loop_body(nb, carry):
            block_step(nb)
            return carry
        lax.fori_loop(0, n_blocks, loop_body, 0)

    for hh in range(hps):
        sl = slice(hh * HEAD_D, (hh + 1) * HEAD_D)
        if emit_state:
            for d in range(2):
                st_ref[0, d, hh] = s_ref[2 * hh + d]
        o = acc_ref[0, :, sl] + acc_ref[1, :, sl]
        o = o * lax.rsqrt(jnp.mean(o * o, axis=-1, keepdims=True) + RMS_EPS) * nw_ref[...]
        o_ref[:, sl] = (o * _silu(z_ref[:, sl])).astype(BF16)


def _gdn(p_main, p_bd, conv_w, alog_vec, dt_vec, norm_w, mask_incl, masks_b, s0, *,
         t_len, row_block0, n_seq, emit_state, hps):
    use_state = s0 is not None
    wid = hps * HEAD_D
    cb = lambda col: col // wid
    tok = lambda col: pl.BlockSpec((t_len, wid), lambda b, h: (b + row_block0, h + cb(col)))
    cws = lambda col: pl.BlockSpec((4, wid), lambda b, h: (0, h + cb(col - COL_Q)))
    vec = pl.BlockSpec((1, HEAD_D), lambda b, h: (0, 0))
    state_spec = pl.BlockSpec((1, 2, hps, HEAD_D, HEAD_D), lambda b, h: (b, 0, h, 0, 0))
    in_specs = [tok(COL_Q), tok(COL_K), tok(COL_V), tok(COL_Z),
                pl.BlockSpec((t_len, HEAD_D), lambda b, h: (b + row_block0, 0)),
                cws(COL_Q), cws(COL_K), cws(COL_V), vec, vec, vec,
                pl.BlockSpec(mask_incl.shape, lambda b, h: (0, 0, 0)),
                pl.BlockSpec(masks_b.shape, lambda b, h: (0, 0, 0, 0))]
    args = [p_main, p_main, p_main, p_main, p_bd, conv_w, conv_w, conv_w, alog_vec, dt_vec, norm_w,
            mask_incl, masks_b]
    if use_state:
        in_specs.append(state_spec)
        args.append(s0)
    out_specs = [pl.BlockSpec((t_len, wid), lambda b, h: (b, h))]
    out_shape = [jax.ShapeDtypeStruct((n_seq * t_len, DN_QK), BF16)]
    if emit_state:
        out_specs.append(state_spec)
        out_shape.append(jax.ShapeDtypeStruct((n_seq, 2, HEADS, HEAD_D, HEAD_D), F32))
    scratch = ([pltpu.VMEM((t_len, wid), F32)] * 3
               + [pltpu.VMEM((2, t_len, wid), F32),
                  pltpu.VMEM((t_len, HEAD_D), F32),
                  pltpu.VMEM((2, t_len, wid), F32)]
               + [pltpu.VMEM((2 * hps, HEAD_D, HEAD_D), F32),
                  pltpu.VMEM((2 * hps, SUPER, HEAD_D), F32)])
    return pl.pallas_call(
        functools.partial(_gdn_kernel, t_len=t_len, use_state=use_state, emit_state=emit_state, hps=hps),
        grid=(n_seq, HEADS // hps),
        in_specs=in_specs, out_specs=out_specs, out_shape=out_shape,
        scratch_shapes=scratch,
        compiler_params=_cparams(2),
        name=f"gated_delta_t{t_len}",
    )(*args)


def _merge_kernel(ac_ref, al_ref, bc_ref, bl_ref, wl_ref, wd_ref, g0_ref, g1_ref, o_ref,
                  wlb_ref, wdb_ref, *, n_ctx_tiles):
    i = pl.program_id(1)

    @pl.when(i == 0)
    def _():
        wlb_ref[...] = wl_ref[...].astype(BF16)
        wdb_ref[...] = wd_ref[...].astype(BF16)

    def finish(a, b):
        pl_ = _dot(a, wlb_ref[...])
        pd_ = _dot(b, wdb_ref[...])
        o_ref[...] = (g0_ref[...].astype(F32) * pl_ + g1_ref[...].astype(F32) * pd_).astype(BF16)

    @pl.when(i < n_ctx_tiles)
    def _():
        finish(ac_ref[...], bc_ref[...])

    @pl.when(i >= n_ctx_tiles)
    def _():
        finish(al_ref[...], bl_ref[...])


def _merge(lru_ctx, lru_lat, dn_ctx, dn_lat, w_lru, w_dn, gates):
    tm = 512
    nj = D_MODEL // TN
    n_ctx = M_CTX // tm
    ctx = pl.BlockSpec((tm, D_LRU), lambda j, i: (jnp.minimum(i, n_ctx - 1), 0))
    lat = pl.BlockSpec((tm, D_LRU), lambda j, i: (jnp.maximum(i - n_ctx, 0), 0))
    return pl.pallas_call(
        functools.partial(_merge_kernel, n_ctx_tiles=n_ctx),
        grid=(nj, M_TOK // tm),
        in_specs=[ctx, lat, ctx, lat,
                  pl.BlockSpec((D_LRU, TN), lambda j, i: (0, j)),
                  pl.BlockSpec((DN_QK, TN), lambda j, i: (0, j)),
                  pl.BlockSpec((tm, TN), lambda j, i: (i, j)),
                  pl.BlockSpec((tm, TN), lambda j, i: (i, j + nj))],
        out_specs=pl.BlockSpec((tm, TN), lambda j, i: (i, j)),
        out_shape=jax.ShapeDtypeStruct((M_TOK, D_MODEL), BF16),
        scratch_shapes=[pltpu.VMEM((D_LRU, TN), BF16), pltpu.VMEM((DN_QK, TN), BF16)],
        compiler_params=_cparams(2),
        name="branch_merge",
    )(lru_ctx, lru_lat, dn_ctx, dn_lat, w_lru, w_dn, gates, gates)


def _layer_norm(x, g, b):
    mu = jnp.mean(x, axis=-1, keepdims=True)
    xc = x - mu
    var = jnp.mean(xc * xc, axis=-1, keepdims=True)
    return xc * lax.rsqrt(var + LN_EPS) * g + b


def _ln1_kernel(xp_ref, xs_ref, pos_ref, mix_ref, gm_ref, shf_ref, scf_ref, g_ref, b_ref, x1_ref, h2_ref,
                *, n_ctx_tiles):
    i = pl.program_id(0)

    def finish(x):
        x1 = _layer_norm(ALPHA * x + gm_ref[0] * mix_ref[...], g_ref[...], b_ref[...])
        x1_ref[...] = x1
        h2_ref[...] = (x1 * (1.0 + scf_ref[0]) + shf_ref[0]).astype(BF16)

    @pl.when(i < n_ctx_tiles)
    def _():
        finish(xp_ref[...])

    @pl.when(i >= n_ctx_tiles)
    def _():
        finish(xs_ref[...] + pos_ref[...])


def _ln1(xp, xs, pos, mix, mods3, g, b):
    tm = 128
    row = pl.BlockSpec((tm, D_MODEL), lambda i: (i, 0))
    vec = pl.BlockSpec((1, D_MODEL), lambda i: (0, 0))
    return pl.pallas_call(
        functools.partial(_ln1_kernel, n_ctx_tiles=M_CTX // tm),
        grid=(M_TOK // tm,),
        in_specs=_token_specs(tm) + [row, _mod_spec(tm, 2), _mod_spec(tm, 3), _mod_spec(tm, 4), vec, vec],
        out_specs=[row, row],
        out_shape=[jax.ShapeDtypeStruct((M_TOK, D_MODEL), F32), jax.ShapeDtypeStruct((M_TOK, D_MODEL), BF16)],
        compiler_params=_cparams(1),
        name="residual_ln1",
    )(xp, xs, pos, mix, mods3, mods3, mods3, g, b)


def _ln2_kernel(x1_ref, ffn_ref, gf_ref, g_ref, b_ref, y_ref):
    y_ref[...] = _layer_norm(ALPHA * x1_ref[...] + gf_ref[0] * ffn_ref[...], g_ref[...], b_ref[...])


def _ln2(x1, ffn, mods3, g, b, *, row0, n_rows):
    tm = 128
    t0 = row0 // tm
    row = pl.BlockSpec((tm, D_MODEL), lambda i: (i + t0, 0))
    vec = pl.BlockSpec((1, D_MODEL), lambda i: (0, 0))
    return pl.pallas_call(
        _ln2_kernel,
        grid=(n_rows // tm,),
        in_specs=[row, row, pl.BlockSpec((1, 1, D_MODEL), lambda i: (_row_group(i + t0, tm), 0, 5)), vec, vec],
        out_specs=pl.BlockSpec((tm, D_MODEL), lambda i: (i, 0)),
        out_shape=jax.ShapeDtypeStruct((n_rows, D_MODEL), F32),
        compiler_params=_cparams(1),
        name=f"residual_ln2_r{row0}",
    )(x1, ffn, mods3, g, b)


def _grid_pos_embed():
    t = np.arange(LAT_T)
    row = (t // GRID_W).astype(np.float32)
    col = (t % GRID_W).astype(np.float32)
    quarter = D_MODEL // 4
    omega = 1.0 / (POS_BASE ** (jnp.arange(quarter, dtype=F32) / quarter))
    er = jnp.asarray(row)[:, None] * omega
    ec = jnp.asarray(col)[:, None] * omega
    return jnp.concatenate([jnp.sin(er), jnp.cos(er), jnp.sin(ec), jnp.cos(ec)], axis=-1)


def _lane_vec(v2x16):
    return jnp.zeros((1, HEAD_D), F32).at[0, 2 * HEADS:4 * HEADS].set(v2x16.reshape(-1))


def kernel(x_prompt, x_sample, state_lru, state_dn, c, c_ctx, w_mod, b_mod, w_in, lru_conv_w, lru_conv_b, lru_gate_w, lru_gate_b, lru_lambda, dn_conv_w, dn_a_log, dn_dt_bias, dn_norm_w, b_branch, w_lru_proj, w_dn_proj, w_o, ln1_g, ln1_b, w_up, w_down, ln2_g, ln2_b):
    xp = x_prompt.reshape(M_CTX, D_MODEL)
    xs = x_sample.reshape(M_LAT, D_MODEL)
    pos = _grid_pos_embed()

    cc = jnp.zeros((8, D_MODEL), F32).at[0].set(c_ctx).at[1:1 + N_LAT_SEQ].set(c)
    mods = _modulation(cc, w_mod[0], b_mod)
    mods3 = mods.reshape(8, 1, 6 * D_MODEL)

    h = _prep(xp, xs, pos, mods3)

    w_in_t = jnp.swapaxes(w_in, 1, 2)[0]
    p_main = _mm(h, w_in_t, col0=0, n_cols=N_MAIN, tn=TN, out_dtype=F32, w_transposed=True, name="in_proj_main")
    p_bd = _mm(h, w_in_t, col0=COL_BD, n_cols=HEAD_D, tn=HEAD_D, out_dtype=F32, w_transposed=True,
               name="in_proj_beta_decay")
    gates = _mm(h, w_in_t, col0=COL_GATE, n_cols=2 * D_MODEL, tn=TN, out_dtype=BF16, w_transposed=True,
                sigmoid_bias=b_branch.reshape(1, 2 * D_MODEL), name="in_proj_gates")

    gate_b4 = lru_gate_b[0].reshape(4, D_LRU)
    lru_args = (lru_conv_w[0], lru_conv_b, lru_gate_w[0], gate_b4, lru_lambda[0])
    lru_ctx, lru_state = _lru(p_main, *lru_args, jnp.zeros((N_CTX_SEQ, 2, D_LRU), F32),
                              t_len=CTX_T, row_block0=0, n_seq=N_CTX_SEQ)
    lru_lat, _ = _lru(p_main, *lru_args, state_lru[:, 0],
                      t_len=LAT_T, row_block0=M_CTX // LAT_T, n_seq=N_LAT_SEQ)

    masks = _dn_masks()
    dn_args = (dn_conv_w[0], _lane_vec(dn_a_log[0]), _lane_vec(dn_dt_bias[0]), dn_norm_w,
               jnp.asarray(masks[:, MASKB_INCL]), jnp.asarray(masks, dtype=BF16))
    dn_ctx, dn_state = _gdn(p_main, p_bd, *dn_args, None,
                            t_len=CTX_T, row_block0=0, n_seq=N_CTX_SEQ, emit_state=True, hps=DN_HEADS_PER_STEP_CTX)
    (dn_lat,) = _gdn(p_main, p_bd, *dn_args, state_dn[:, 0],
                     t_len=LAT_T, row_block0=M_CTX // LAT_T, n_seq=N_LAT_SEQ, emit_state=False, hps=DN_HEADS_PER_STEP_LAT)

    merged = _merge(lru_ctx, lru_lat, dn_ctx, dn_lat, w_lru_proj[0], w_dn_proj[0], gates)
    mix = _mm(merged, w_o[0], col0=0, n_cols=D_MODEL, tn=TN, out_dtype=F32, name="out_proj")

    x1, h2 = _ln1(xp, xs, pos, mix, mods3, ln1_g, ln1_b)
    u = _mm(h2, w_up[0], col0=0, n_cols=D_FF, tn=TN, out_dtype=BF16, relu2=True, name="up_proj")
    ffn = _down(u, w_down[0])

    y_ctx = _ln2(x1, ffn, mods3, ln2_g, ln2_b, row0=0, n_rows=M_CTX)
    y_lat = _ln2(x1, ffn, mods3, ln2_g, ln2_b, row0=M_CTX, n_rows=M_LAT)

    return (y_ctx.reshape(x_prompt.shape), y_lat.reshape(x_sample.shape),
            lru_state.reshape(N_CTX_SEQ, 1, 2, D_LRU),
            dn_state.reshape(N_CTX_SEQ, 1, 2, HEADS, HEAD_D, HEAD_D))
```

```python
import functools
import math

import jax
import jax.numpy as jnp
import numpy as np
from jax import lax
from jax.experimental import pallas as pl
from jax.experimental.pallas import tpu as pltpu

F32 = jnp.float32
BF16 = jnp.bfloat16

D_MODEL = 4096
N_CTX_SEQ, CTX_T = 16, 256
N_LAT_SEQ, LAT_T = 2, 1024
M_CTX = N_CTX_SEQ * CTX_T
M_LAT = N_LAT_SEQ * LAT_T
M_TOK = M_CTX + M_LAT
GRID_W = 64
D_LRU = 2048
LRU_C = 8.0
HEADS = 16
HEAD_D = 128
DN_QK = HEADS * HEAD_D
CHUNK = 64
SUPER = 256
DN_HEADS_PER_STEP_CTX = 4
DN_HEADS_PER_STEP_LAT = 4
D_FF = 4 * D_MODEL
ALPHA = 2.0 ** 0.25
LN_EPS = 1e-5
RMS_EPS = 1e-6
POS_BASE = 10000.0
COL_LRU_X, COL_LRU_Y, COL_Q, COL_K, COL_V, COL_Z = 0, 2048, 4096, 6144, 8192, 10240
COL_BD = 12288
COL_GATE = 12352
N_MAIN = COL_BD
VMEM_LIMIT = 56 * 1024 * 1024
TM = 1536
TN = 512


def _cparams(n_axes):
    return pltpu.CompilerParams(dimension_semantics=("arbitrary",) * n_axes, vmem_limit_bytes=VMEM_LIMIT)


def _row_group(i, tm):
    n_ctx = M_CTX // tm
    return jnp.where(i < n_ctx, 0, 1 + (i - n_ctx) // (LAT_T // tm))


def _mod_spec(tm, k):
    return pl.BlockSpec((1, 1, D_MODEL), lambda i: (_row_group(i, tm), 0, k))


def _sigmoid(x):
    return 0.5 * (1.0 + jnp.tanh(0.5 * x))


def _silu(x):
    return x * _sigmoid(x)


def _softplus(x):
    return jnp.maximum(x, 0.0) + jnp.log1p(jnp.exp(-jnp.abs(x)))


def _dot(a, b):
    return jnp.dot(a, b, preferred_element_type=F32)


def _mod_kernel(c_ref, w_ref, b_ref, o_ref):
    s = _silu(c_ref[...]).astype(BF16)
    o_ref[...] = _dot(s, w_ref[...].astype(BF16)) + b_ref[...]


def _modulation(cc, w_mod, b_mod):
    tn = 512
    n = w_mod.shape[1]
    return pl.pallas_call(
        _mod_kernel,
        grid=(n // tn,),
        in_specs=[pl.BlockSpec((8, D_MODEL), lambda j: (0, 0)),
                  pl.BlockSpec((D_MODEL, tn), lambda j: (0, j)),
                  pl.BlockSpec((1, tn), lambda j: (0, j))],
        out_specs=pl.BlockSpec((8, tn), lambda j: (0, j)),
        out_shape=jax.ShapeDtypeStruct((8, n), F32),
        compiler_params=_cparams(1),
        name="modulation",
    )(cc, w_mod, b_mod)


def _token_specs(tm):
    n_ctx = M_CTX // tm
    n_pos = LAT_T // tm
    return [pl.BlockSpec((tm, D_MODEL), lambda i: (jnp.minimum(i, n_ctx - 1), 0)),
            pl.BlockSpec((tm, D_MODEL), lambda i: (jnp.maximum(i - n_ctx, 0), 0)),
            pl.BlockSpec((tm, D_MODEL), lambda i: (jnp.maximum(i - n_ctx, 0) % n_pos, 0))]


def _prep_kernel(xp_ref, xs_ref, pos_ref, sh_ref, sc_ref, h_ref, *, n_ctx_tiles):
    i = pl.program_id(0)
    sc = 1.0 + sc_ref[0]
    sh = sh_ref[0]

    @pl.when(i < n_ctx_tiles)
    def _():
        h_ref[...] = (xp_ref[...] * sc + sh).astype(BF16)

    @pl.when(i >= n_ctx_tiles)
    def _():
        h_ref[...] = ((xs_ref[...] + pos_ref[...]) * sc + sh).astype(BF16)


def _prep(xp, xs, pos, mods3):
    tm = 256
    return pl.pallas_call(
        functools.partial(_prep_kernel, n_ctx_tiles=M_CTX // tm),
        grid=(M_TOK // tm,),
        in_specs=_token_specs(tm) + [_mod_spec(tm, 0), _mod_spec(tm, 1)],
        out_specs=pl.BlockSpec((tm, D_MODEL), lambda i: (i, 0)),
        out_shape=jax.ShapeDtypeStruct((M_TOK, D_MODEL), BF16),
        compiler_params=_cparams(1),
        name="prep",
    )(xp, xs, pos, mods3, mods3)


def _mm_kernel(lhs_ref, w_ref, *rest, relu2, sigmoid_bias, w_transposed, kc):
    if sigmoid_bias:
        bias_ref, o_ref, wb_ref = rest
    else:
        o_ref, wb_ref = rest
    k_dim = lhs_ref.shape[1]

    def finish(acc):
        if relu2:
            acc = jnp.maximum(acc, 0.0)
            acc = acc * acc
        if sigmoid_bias:
            acc = _sigmoid(acc + bias_ref[...])
        o_ref[...] = acc.astype(o_ref.dtype)

    @pl.when(pl.program_id(1) == 0)
    def _():
        acc = None
        for c in range(k_dim // kc):
            ks = slice(c * kc, (c + 1) * kc)
            wb = (w_ref[:, ks].T if w_transposed else w_ref[ks, :]).astype(BF16)
            wb_ref[ks, :] = wb
            part = _dot(lhs_ref[:, ks], wb)
            acc = part if acc is None else acc + part
        finish(acc)

    @pl.when(pl.program_id(1) > 0)
    def _():
        finish(_dot(lhs_ref[...], wb_ref[...]))


def _mm(lhs, w, *, col0, n_cols, tn, out_dtype, relu2=False, sigmoid_bias=None, w_transposed=False, name):
    m, k = lhs.shape
    if w_transposed:
        assert col0 % 8 == 0 and tn % 8 == 0
        w_spec = pl.BlockSpec((pl.Element(tn), pl.Element(k)), lambda j, i: (pl.multiple_of(col0 + j * tn, 8), 0))
    else:
        w_spec = pl.BlockSpec((k, tn), lambda j, i: (0, col0 // tn + j))
    in_specs = [pl.BlockSpec((TM, k), lambda j, i: (i, 0)), w_spec]
    args = [lhs, w]
    if sigmoid_bias is not None:
        in_specs.append(pl.BlockSpec((1, tn), lambda j, i: (0, j)))
        args.append(sigmoid_bias)
    return pl.pallas_call(
        functools.partial(_mm_kernel, relu2=relu2, sigmoid_bias=sigmoid_bias is not None,
                          w_transposed=w_transposed, kc=512),
        grid=(n_cols // tn, m // TM),
        in_specs=in_specs,
        out_specs=pl.BlockSpec((TM, tn), lambda j, i: (i, j)),
        out_shape=jax.ShapeDtypeStruct((m, n_cols), out_dtype),
        scratch_shapes=[pltpu.VMEM((k, tn), BF16)],
        compiler_params=_cparams(2),
        name=name,
    )(*args)


def _down_kernel(lhs_ref, w_ref, o_ref, wb_ref, *, tm, kc):
    k = pl.program_id(1)
    i = pl.program_id(2)
    rows = pl.ds(pl.multiple_of(i * tm, tm), tm)
    tk = lhs_ref.shape[1]

    def accumulate(acc):
        @pl.when(k == 0)
        def _():
            o_ref[rows, :] = acc

        @pl.when(k > 0)
        def _():
            o_ref[rows, :] += acc

    @pl.when(i == 0)
    def _():
        acc = None
        for c in range(tk // kc):
            ks = slice(c * kc, (c + 1) * kc)
            wb = w_ref[ks, :].astype(BF16)
            wb_ref[ks, :] = wb
            part = _dot(lhs_ref[:, ks], wb)
            acc = part if acc is None else acc + part
        accumulate(acc)

    @pl.when(i > 0)
    def _():
        accumulate(_dot(lhs_ref[...], wb_ref[...]))


def _down(u, w_down):
    tk, tm = 4096, 1024
    m = u.shape[0]
    return pl.pallas_call(
        functools.partial(_down_kernel, tm=tm, kc=512),
        grid=(D_MODEL // TN, D_FF // tk, m // tm),
        in_specs=[pl.BlockSpec((tm, tk), lambda j, k, i: (i, k)),
                  pl.BlockSpec((tk, TN), lambda j, k, i: (k, j))],
        out_specs=pl.BlockSpec((m, TN), lambda j, k, i: (0, j), pipeline_mode=pl.Buffered(1)),
        out_shape=jax.ShapeDtypeStruct((m, D_MODEL), F32),
        scratch_shapes=[pltpu.VMEM((tk, TN), BF16)],
        compiler_params=_cparams(3),
        name="down_proj",
    )(u, w_down)


def _conv4(x, w):
    t = x.shape[0]
    row = lax.broadcasted_iota(jnp.int32, x.shape, 0)
    xm2 = jnp.where(row >= 2, pltpu.roll(x, 2, 0), 0.0)
    xm1 = jnp.where(row >= 1, pltpu.roll(x, 1, 0), 0.0)
    xp1 = jnp.where(row < t - 1, pltpu.roll(x, t - 1, 0), 0.0)
    return w[0:1] * xm2 + w[1:2] * xm1 + w[2:3] * x + w[3:4] * xp1


def _lru_kernel(x_ref, y_ref, cw_ref, cb_ref, gw_ref, gb_ref, lam_ref, h0_ref, o_ref, st_ref,
                af_ref, bf_ref, ab_ref, bb_ref, *, t_len, width):
    for n in range(width // HEAD_D):
        sl = slice(n * HEAD_D, (n + 1) * HEAD_D)
        xc = _conv4(x_ref[:, sl], cw_ref[:, sl]) + cb_ref[:, sl]
        xcb = xc.astype(BF16)
        for d, (a_ref, b_ref) in enumerate(((af_ref, bf_ref), (ab_ref, bb_ref))):
            pre_r = _dot(xcb, gw_ref[d, 0, n].astype(BF16))
            pre_i = _dot(xcb, gw_ref[d, 1, n].astype(BF16))
            r = _sigmoid(pre_r + gb_ref[2 * d:2 * d + 1, sl])
            ig = _sigmoid(pre_i + gb_ref[2 * d + 1:2 * d + 2, sl])
            log_a = (-LRU_C) * r * _softplus(-lam_ref[d:d + 1, sl])
            a = jnp.exp(log_a)
            mult = jnp.sqrt((1.0 + a * a) * jnp.tanh(-log_a))
            a_ref[:, sl] = a
            b_ref[:, sl] = mult * (ig * xc)

    n_tiles = t_len // 8
    rowi = lax.broadcasted_iota(jnp.int32, (8, width), 0)

    def body(g, carry):
        cf, cb = carry
        r0 = pl.multiple_of(g * 8, 8)
        a8 = af_ref[pl.ds(r0, 8), :]
        b8 = bf_ref[pl.ds(r0, 8), :]
        for dd in (1, 2, 4):
            m = rowi >= dd
            a_sh = jnp.where(m, pltpu.roll(a8, dd, 0), 1.0)
            b_sh = jnp.where(m, pltpu.roll(b8, dd, 0), 0.0)
            b8 = a8 * b_sh + b8
            a8 = a8 * a_sh
        h8 = a8 * cf + b8
        bf_ref[pl.ds(r0, 8), :] = h8
        cf = h8[7:8, :]

        r1 = pl.multiple_of((n_tiles - 1 - g) * 8, 8)
        a8 = ab_ref[pl.ds(r1, 8), :]
        b8 = bb_ref[pl.ds(r1, 8), :]
        for dd in (1, 2, 4):
            m = rowi < 8 - dd
            a_sh = jnp.where(m, pltpu.roll(a8, 8 - dd, 0), 1.0)
            b_sh = jnp.where(m, pltpu.roll(b8, 8 - dd, 0), 0.0)
            b8 = a8 * b_sh + b8
            a8 = a8 * a_sh
        h8 = a8 * cb + b8
        bb_ref[pl.ds(r1, 8), :] = h8
        cb = h8[0:1, :]
        return cf, cb

    cf, cb = lax.fori_loop(0, n_tiles, body, (h0_ref[0, 0:1, :], h0_ref[0, 1:2, :]))
    st_ref[0, 0:1, :] = cf
    st_ref[0, 1:2, :] = cb
    y = y_ref[...]
    gelu = 0.5 * y * (1.0 + jnp.tanh(math.sqrt(2.0 / math.pi) * (y + 0.044715 * (y * y * y))))
    o_ref[...] = ((bf_ref[...] + bb_ref[...]) * gelu).astype(BF16)


def _lru(p_main, conv_w, conv_b, gate_w, gate_b4, lam, h0, *, t_len, row_block0, n_seq):
    width = 512
    nblk = width // HEAD_D
    ncb = D_LRU // width
    return pl.pallas_call(
        functools.partial(_lru_kernel, t_len=t_len, width=width),
        grid=(n_seq, ncb),
        in_specs=[pl.BlockSpec((t_len, width), lambda b, c: (b + row_block0, c)),
                  pl.BlockSpec((t_len, width), lambda b, c: (b + row_block0, c + COL_LRU_Y // width)),
                  pl.BlockSpec((4, width), lambda b, c: (0, c)),
                  pl.BlockSpec((1, width), lambda b, c: (0, c)),
                  pl.BlockSpec((2, 2, nblk, HEAD_D, HEAD_D), lambda b, c: (0, 0, c, 0, 0)),
                  pl.BlockSpec((4, width), lambda b, c: (0, c)),
                  pl.BlockSpec((2, width), lambda b, c: (0, c)),
                  pl.BlockSpec((1, 2, width), lambda b, c: (b, 0, c))],
        out_specs=[pl.BlockSpec((t_len, width), lambda b, c: (b, c)),
                   pl.BlockSpec((1, 2, width), lambda b, c: (b, 0, c))],
        out_shape=[jax.ShapeDtypeStruct((n_seq * t_len, D_LRU), BF16),
                   jax.ShapeDtypeStruct((n_seq, 2, D_LRU), F32)],
        scratch_shapes=[pltpu.VMEM((t_len, width), F32)] * 4,
        compiler_params=_cparams(2),
        name=f"rg_lru_t{t_len}",
    )(p_main, p_main, conv_w, conv_b, gate_w, gate_b4, lam, h0)


N_LEVELS = 6
MASKB_INCL, MASKB_LEVEL0, MASKB_EYE = 0, 1, 1 + N_LEVELS


def _dn_masks():
    i = np.arange(SUPER)[:, None]
    j = np.arange(SUPER)[None, :]
    same = (i // CHUNK) == (j // CHUNK)
    out = np.zeros((2, MASKB_EYE + 1, SUPER, SUPER), np.float32)
    for d in range(2):
        before = (j > i) if d == 1 else (j < i)
        out[d, MASKB_INCL] = same & (before | (i == j))
        s = 1
        for lv in range(N_LEVELS):
            blk = (i // (2 * s)) == (j // (2 * s))
            il, jl = i % (2 * s), j % (2 * s)
            out[d, MASKB_LEVEL0 + lv] = blk & (((il < s) & (jl >= s)) if d == 1 else ((il >= s) & (jl < s)))
            s *= 2
        out[d, MASKB_EYE] = (i == j)
    return out


def _dn_block(chains, g_rows, m_ref, mb_ref, s_ref, vn_ref):
    nt = (((1,), (1,)), ((), ()))
    tn = (((0,), (0,)), ((), ()))
    n_chunks = SUPER // CHUNK
    gam_all = []
    for d in range(2):
        incl_b = mb_ref[d, MASKB_INCL]
        g_hi = g_rows[d].astype(BF16)
        r = g_rows[d] - g_hi.astype(F32)
        g_mid = r.astype(BF16)
        g_lo = (r - g_mid.astype(F32)).astype(BF16)
        gam_all.append(_dot(incl_b, g_hi) + (_dot(incl_b, g_mid) + _dot(incl_b, g_lo)))
    lane = lax.broadcasted_iota(jnp.int32, (SUPER, HEAD_D), 1)
    st = []
    for q, k, v, beta_b, g_lane, d, idx in chains:
        gam_col = jnp.sum(jnp.where(lane == g_lane, gam_all[d], 0.0), axis=1, keepdims=True)
        gam1 = jnp.broadcast_to(gam_col, (SUPER, HEAD_D))
        kb = k * beta_b
        kbf = k.astype(BF16)
        kk = lax.dot_general(kb.astype(BF16), kbf, nt, preferred_element_type=F32)
        qk = lax.dot_general(q.astype(BF16), kbf, nt, preferred_element_type=F32)
        st.append(dict(q=q, k=k, v=v, beta_b=beta_b, d=d, idx=idx, gam1=gam1, gam_col=gam_col, kb=kb, kk=kk, qk=qk))
    for c in st:
        d = c["d"]
        gam = jnp.broadcast_to(c.pop("gam_col"), (SUPER, SUPER))
        decay = jnp.exp(jnp.minimum(gam - gam.T, 0.0)) * m_ref[d]
        c["lmat"] = (c.pop("kk") * decay).astype(BF16)
        c["attn"] = (c.pop("qk") * decay).astype(BF16)
        c["minv"] = mb_ref[d, MASKB_EYE] - c["lmat"] * mb_ref[d, MASKB_LEVEL0]

    for lv in range(1, N_LEVELS):
        for c in st:
            c["x"] = _dot(c["lmat"] * mb_ref[c["d"], MASKB_LEVEL0 + lv], c["minv"]).astype(BF16)
        for c in st:
            c["minv"] = c["minv"] - _dot(c["minv"], c.pop("x")).astype(BF16)
    for c in st:
        eg = jnp.exp(c["gam1"])
        rhs = jnp.concatenate([c["v"] * c["beta_b"], c["kb"] * eg], axis=1)
        uw = _dot(c.pop("minv"), rhs.astype(BF16))
        c["u"], c["w"] = uw[:, :HEAD_D], uw[:, HEAD_D:].astype(BF16)
        c["q_dec"] = (c["q"] * eg).astype(BF16)
        lasts = []
        for n in range(n_chunks):
            r = n * CHUNK if c["d"] == 1 else n * CHUNK + CHUNK - 1
            lasts.append(jnp.broadcast_to(c["gam1"][r:r + 1, :], (CHUNK, HEAD_D)))
        c["g_last"] = jnp.concatenate(lasts, axis=0)
        c["k_dec"] = (c["k"] * jnp.exp(c["g_last"] - c["gam1"])).astype(BF16)
        c["outs"] = [None] * n_chunks
        vn_ref[c["idx"]] = jnp.zeros((SUPER, HEAD_D), F32)

    for step in range(n_chunks):
        for c in st:
            n = n_chunks - 1 - step if c["d"] == 1 else step
            rows = slice(n * CHUNK, (n + 1) * CHUNK)
            c["s"] = s_ref[c["idx"]]
            c["sb"] = c["s"].astype(BF16)
            c["v_new"] = c["u"][rows] - _dot(c["w"][rows], c["sb"])
            vn_ref[c["idx"], rows, :] = c["v_new"]
        for c in st:
            n = n_chunks - 1 - step if c["d"] == 1 else step
            rows = slice(n * CHUNK, (n + 1) * CHUNK)
            c["outs"][n] = (_dot(c["q_dec"][rows], c["sb"])
                            + _dot(c["attn"][rows], vn_ref[c["idx"]].astype(BF16)))
            inc = lax.dot_general(c["k_dec"][rows], c["v_new"].astype(BF16), tn, preferred_element_type=F32)
            s_ref[c["idx"]] = c["s"] * jnp.exp(c["g_last"][n * CHUNK:n * CHUNK + 1, :]) + inc
    return [jnp.concatenate(c["outs"], axis=0) for c in st]


def _gdn_kernel(*refs, t_len, use_state, emit_state, hps):
    (q_ref, k_ref, v_ref, z_ref, bd_ref, cwq_ref, cwk_ref, cwv_ref, alog_ref, dt_ref, nw_ref,
     m_ref, mb_ref), refs = refs[:13], refs[13:]
    if use_state:
        s0_ref, refs = refs[0], refs[1:]
    o_ref, refs = refs[0], refs[1:]
    if emit_state:
        st_ref, refs = refs[0], refs[1:]
    qs_ref, ks_ref, vs_ref, bet_ref, gall_ref, acc_ref, s_ref, vn_ref = refs

    head0 = pl.program_id(1) * hps

    def l2n(x):
        return x * lax.rsqrt(jnp.sum(x * x, axis=-1, keepdims=True) + RMS_EPS)

    bd = bd_ref[...]
    beta_all = _sigmoid(bd)
    gall_ref[...] = -jnp.exp(alog_ref[...]) * _softplus(bd + dt_ref[...])
    lane = lax.broadcasted_iota(jnp.int32, bd.shape, 1)
    for hh in range(hps):
        sl = slice(hh * HEAD_D, (hh + 1) * HEAD_D)
        qs_ref[:, sl] = l2n(_silu(_conv4(q_ref[:, sl], cwq_ref[:, sl]))) * (HEAD_D ** -0.5)
        ks_ref[:, sl] = l2n(_silu(_conv4(k_ref[:, sl], cwk_ref[:, sl])))
        vs_ref[:, sl] = _silu(_conv4(v_ref[:, sl], cwv_ref[:, sl]))
        for d in range(2):
            col = d * HEADS + head0 + hh
            beta = jnp.sum(jnp.where(lane == col, beta_all, 0.0), axis=1, keepdims=True)
            bet_ref[d, :, sl] = jnp.broadcast_to(beta, (t_len, HEAD_D))
            if use_state:
                s_ref[2 * hh + d] = s0_ref[0, d, hh]
            else:
                s_ref[2 * hh + d] = jnp.zeros((HEAD_D, HEAD_D), F32)

    n_blocks = t_len // SUPER

    def block_step(nb):
        chains, dests, g_rows = [], [], []
        for d in range(2):
            blk = nb if d == 0 else n_blocks - 1 - nb
            g_rows.append(gall_ref[pl.ds(pl.multiple_of(blk * SUPER, SUPER), SUPER), :])
        for hh in range(hps):
            sl = slice(hh * HEAD_D, (hh + 1) * HEAD_D)
            for d in range(2):
                blk = nb if d == 0 else n_blocks - 1 - nb
                rows = pl.ds(pl.multiple_of(blk * SUPER, SUPER), SUPER)
                chains.append((qs_ref[rows, sl], ks_ref[rows, sl], vs_ref[rows, sl], bet_ref[d, rows, sl],
                               2 * HEADS + d * HEADS + head0 + hh, d, 2 * hh + d))
                dests.append((d, rows, sl))
        for (d, rows, sl), out in zip(dests, _dn_block(chains, g_rows, m_ref, mb_ref, s_ref, vn_ref)):
            acc_ref[d, rows, sl] = out

    if n_blocks == 1:
        block_step(0)
    else:
        def loop_body(nb, carry):
            block_step(nb)
            return carry
        lax.fori_loop(0, n_blocks, loop_body, 0)

    for hh in range(hps):
        sl = slice(hh * HEAD_D, (hh + 1) * HEAD_D)
        if emit_state:
            for d in range(2):
                st_ref[0, d, hh] = s_ref[2 * hh + d]
        o = acc_ref[0, :, sl] + acc_ref[1, :, sl]
        o = o * lax.rsqrt(jnp.mean(o * o, axis=-1, keepdims=True) + RMS_EPS) * nw_ref[...]
        o_ref[:, sl] = (o * _silu(z_ref[:, sl])).astype(BF16)


def _gdn(p_main, p_bd, conv_w, alog_vec, dt_vec, norm_w, mask_incl, masks_b, s0, *,
         t_len, row_block0, n_seq, emit_state, hps):
    use_state = s0 is not None
    wid = hps * HEAD_D
    cb = lambda col: col // wid
    buffering = dict(pipeline_mode=pl.Buffered(1)) if t_len > CTX_T else {}
    tok = lambda col: pl.BlockSpec((t_len, wid), lambda b, h: (b + row_block0, h + cb(col)), **buffering)
    cws = lambda col: pl.BlockSpec((4, wid), lambda b, h: (0, h + cb(col - COL_Q)))
    vec = pl.BlockSpec((1, HEAD_D), lambda b, h: (0, 0))
    state_spec = pl.BlockSpec((1, 2, hps, HEAD_D, HEAD_D), lambda b, h: (b, 0, h, 0, 0))
    in_specs = [tok(COL_Q), tok(COL_K), tok(COL_V), tok(COL_Z),
                pl.BlockSpec((t_len, HEAD_D), lambda b, h: (b + row_block0, 0)),
                cws(COL_Q), cws(COL_K), cws(COL_V), vec, vec, vec,
                pl.BlockSpec(mask_incl.shape, lambda b, h: (0, 0, 0)),
                pl.BlockSpec(masks_b.shape, lambda b, h: (0, 0, 0, 0))]
    args = [p_main, p_main, p_main, p_main, p_bd, conv_w, conv_w, conv_w, alog_vec, dt_vec, norm_w,
            mask_incl, masks_b]
    if use_state:
        in_specs.append(state_spec)
        args.append(s0)
    out_specs = [pl.BlockSpec((t_len, wid), lambda b, h: (b, h))]
    out_shape = [jax.ShapeDtypeStruct((n_seq * t_len, DN_QK), BF16)]
    if emit_state:
        out_specs.append(state_spec)
        out_shape.append(jax.ShapeDtypeStruct((n_seq, 2, HEADS, HEAD_D, HEAD_D), F32))
    scratch = ([pltpu.VMEM((t_len, wid), F32)] * 3
               + [pltpu.VMEM((2, t_len, wid), F32),
                  pltpu.VMEM((t_len, HEAD_D), F32),
                  pltpu.VMEM((2, t_len, wid), F32)]
               + [pltpu.VMEM((2 * hps, HEAD_D, HEAD_D), F32),
                  pltpu.VMEM((2 * hps, SUPER, HEAD_D), F32)])
    return pl.pallas_call(
        functools.partial(_gdn_kernel, t_len=t_len, use_state=use_state, emit_state=emit_state, hps=hps),
        grid=(n_seq, HEADS // hps),
        in_specs=in_specs, out_specs=out_specs, out_shape=out_shape,
        scratch_shapes=scratch,
        compiler_params=_cparams(2),
        name=f"gated_delta_t{t_len}",
    )(*args)


def _merge_kernel(ac_ref, al_ref, bc_ref, bl_ref, wl_ref, wd_ref, g0_ref, g1_ref, o_ref,
                  wlb_ref, wdb_ref, *, n_ctx_tiles, kc):
    i = pl.program_id(1)

    def finish(pl_, pd_):
        o_ref[...] = (g0_ref[...].astype(F32) * pl_ + g1_ref[...].astype(F32) * pd_).astype(BF16)

    @pl.when(i == 0)
    def _():
        accs = []
        for a_ref, w_ref, wb_ref in ((ac_ref, wl_ref, wlb_ref), (bc_ref, wd_ref, wdb_ref)):
            acc = None
            for c in range(w_ref.shape[0] // kc):
                ks = slice(c * kc, (c + 1) * kc)
                wb = w_ref[ks, :].astype(BF16)
                wb_ref[ks, :] = wb
                part = _dot(a_ref[:, ks], wb)
                acc = part if acc is None else acc + part
            accs.append(acc)
        finish(*accs)

    @pl.when((i > 0) & (i < n_ctx_tiles))
    def _():
        finish(_dot(ac_ref[...], wlb_ref[...]), _dot(bc_ref[...], wdb_ref[...]))

    @pl.when(i >= n_ctx_tiles)
    def _():
        finish(_dot(al_ref[...], wlb_ref[...]), _dot(bl_ref[...], wdb_ref[...]))


def _merge(lru_ctx, lru_lat, dn_ctx, dn_lat, w_lru, w_dn, gates):
    tm = 1024
    nj = D_MODEL // TN
    n_ctx = M_CTX // tm
    ctx = pl.BlockSpec((tm, D_LRU), lambda j, i: (jnp.minimum(i, n_ctx - 1), 0))
    lat = pl.BlockSpec((tm, D_LRU), lambda j, i: (jnp.maximum(i - n_ctx, 0), 0))
    wspec = pl.BlockSpec((D_LRU, TN), lambda j, i: (0, j), pipeline_mode=pl.Buffered(1))
    return pl.pallas_call(
        functools.partial(_merge_kernel, n_ctx_tiles=n_ctx, kc=512),
        grid=(nj, M_TOK // tm),
        in_specs=[ctx, lat, ctx, lat, wspec, wspec,
                  pl.BlockSpec((tm, TN), lambda j, i: (i, j)),
                  pl.BlockSpec((tm, TN), lambda j, i: (i, j + nj))],
        out_specs=pl.BlockSpec((tm, TN), lambda j, i: (i, j)),
        out_shape=jax.ShapeDtypeStruct((M_TOK, D_MODEL), BF16),
        scratch_shapes=[pltpu.VMEM((D_LRU, TN), BF16), pltpu.VMEM((DN_QK, TN), BF16)],
        compiler_params=_cparams(2),
        name="branch_merge",
    )(lru_ctx, lru_lat, dn_ctx, dn_lat, w_lru, w_dn, gates, gates)


def _layer_norm(x, g, b):
    mu = jnp.mean(x, axis=-1, keepdims=True)
    xc = x - mu
    var = jnp.mean(xc * xc, axis=-1, keepdims=True)
    return xc * lax.rsqrt(var + LN_EPS) * g + b


def _ln1_kernel(xp_ref, xs_ref, pos_ref, mix_ref, gm_ref, shf_ref, scf_ref, g_ref, b_ref, x1_ref, h2_ref,
                *, n_ctx_tiles):
    i = pl.program_id(0)

    def finish(x):
        x1 = _layer_norm(ALPHA * x + gm_ref[0] * mix_ref[...], g_ref[...], b_ref[...])
        x1_ref[...] = x1
        h2_ref[...] = (x1 * (1.0 + scf_ref[0]) + shf_ref[0]).astype(BF16)

    @pl.when(i < n_ctx_tiles)
    def _():
        finish(xp_ref[...])

    @pl.when(i >= n_ctx_tiles)
    def _():
        finish(xs_ref[...] + pos_ref[...])


def _ln1(xp, xs, pos, mix, mods3, g, b):
    tm = 256
    row = pl.BlockSpec((tm, D_MODEL), lambda i: (i, 0))
    vec = pl.BlockSpec((1, D_MODEL), lambda i: (0, 0))
    return pl.pallas_call(
        functools.partial(_ln1_kernel, n_ctx_tiles=M_CTX // tm),
        grid=(M_TOK // tm,),
        in_specs=_token_specs(tm) + [row, _mod_spec(tm, 2), _mod_spec(tm, 3), _mod_spec(tm, 4), vec, vec],
        out_specs=[row, row],
        out_shape=[jax.ShapeDtypeStruct((M_TOK, D_MODEL), F32), jax.ShapeDtypeStruct((M_TOK, D_MODEL), BF16)],
        compiler_params=_cparams(1),
        name="residual_ln1",
    )(xp, xs, pos, mix, mods3, mods3, mods3, g, b)


def _ln2_kernel(x1_ref, ffn_ref, gf_ref, g_ref, b_ref, y_ref):
    y_ref[...] = _layer_norm(ALPHA * x1_ref[...] + gf_ref[0] * ffn_ref[...], g_ref[...], b_ref[...])


def _ln2(x1, ffn, mods3, g, b, *, row0, n_rows):
    tm = 256
    t0 = row0 // tm
    row = pl.BlockSpec((tm, D_MODEL), lambda i: (i + t0, 0))
    vec = pl.BlockSpec((1, D_MODEL), lambda i: (0, 0))
    return pl.pallas_call(
        _ln2_kernel,
        grid=(n_rows // tm,),
        in_specs=[row, row, pl.BlockSpec((1, 1, D_MODEL), lambda i: (_row_group(i + t0, tm), 0, 5)), vec, vec],
        out_specs=pl.BlockSpec((tm, D_MODEL), lambda i: (i, 0)),
        out_shape=jax.ShapeDtypeStruct((n_rows, D_MODEL), F32),
        compiler_params=_cparams(1),
        name=f"residual_ln2_r{row0}",
    )(x1, ffn, mods3, g, b)


def _grid_pos_embed():
    t = np.arange(LAT_T)
    quarter = D_MODEL // 4
    omega = 1.0 / (POS_BASE ** (np.arange(quarter, dtype=np.float64) / quarter))
    er = (t // GRID_W)[:, None] * omega
    ec = (t % GRID_W)[:, None] * omega
    return jnp.asarray(np.concatenate([np.sin(er), np.cos(er), np.sin(ec), np.cos(ec)], axis=-1), dtype=F32)


def _lane_vec(v2x16):
    return jnp.zeros((1, HEAD_D), F32).at[0, 2 * HEADS:4 * HEADS].set(v2x16.reshape(-1))


def kernel(x_prompt, x_sample, state_lru, state_dn, c, c_ctx, w_mod, b_mod, w_in, lru_conv_w, lru_conv_b, lru_gate_w, lru_gate_b, lru_lambda, dn_conv_w, dn_a_log, dn_dt_bias, dn_norm_w, b_branch, w_lru_proj, w_dn_proj, w_o, ln1_g, ln1_b, w_up, w_down, ln2_g, ln2_b):
    xp = x_prompt.reshape(M_CTX, D_MODEL)
    xs = x_sample.reshape(M_LAT, D_MODEL)
    pos = _grid_pos_embed()

    cc = jnp.zeros((8, D_MODEL), F32).at[0].set(c_ctx).at[1:1 + N_LAT_SEQ].set(c)
    mods = _modulation(cc, w_mod[0], b_mod)
    mods3 = mods.reshape(8, 1, 6 * D_MODEL)

    h = _prep(xp, xs, pos, mods3)

    w_in_t = jnp.swapaxes(w_in, 1, 2)[0]
    p_main = _mm(h, w_in_t, col0=0, n_cols=N_MAIN, tn=TN, out_dtype=F32, w_transposed=True, name="in_proj_main")
    p_bd = _mm(h, w_in_t, col0=COL_BD, n_cols=HEAD_D, tn=HEAD_D, out_dtype=F32, w_transposed=True,
               name="in_proj_beta_decay")
    gates = _mm(h, w_in_t, col0=COL_GATE, n_cols=2 * D_MODEL, tn=TN, out_dtype=BF16, w_transposed=True,
                sigmoid_bias=b_branch.reshape(1, 2 * D_MODEL), name="in_proj_gates")

    gate_b4 = lru_gate_b[0].reshape(4, D_LRU)
    lru_args = (lru_conv_w[0], lru_conv_b, lru_gate_w[0], gate_b4, lru_lambda[0])
    lru_ctx, lru_state = _lru(p_main, *lru_args, jnp.zeros((N_CTX_SEQ, 2, D_LRU), F32),
                              t_len=CTX_T, row_block0=0, n_seq=N_CTX_SEQ)
    lru_lat, _ = _lru(p_main, *lru_args, state_lru[:, 0],
                      t_len=LAT_T, row_block0=M_CTX // LAT_T, n_seq=N_LAT_SEQ)

    masks = _dn_masks()
    dn_args = (dn_conv_w[0], _lane_vec(dn_a_log[0]), _lane_vec(dn_dt_bias[0]), dn_norm_w,
               jnp.asarray(masks[:, MASKB_INCL]), jnp.asarray(masks, dtype=BF16))
    dn_ctx, dn_state = _gdn(p_main, p_bd, *dn_args, None,
                            t_len=CTX_T, row_block0=0, n_seq=N_CTX_SEQ, emit_state=True, hps=DN_HEADS_PER_STEP_CTX)
    (dn_lat,) = _gdn(p_main, p_bd, *dn_args, state_dn[:, 0],
                     t_len=LAT_T, row_block0=M_CTX // LAT_T, n_seq=N_LAT_SEQ, emit_state=False, hps=DN_HEADS_PER_STEP_LAT)

    merged = _merge(lru_ctx, lru_lat, dn_ctx, dn_lat, w_lru_proj[0], w_dn_proj[0], gates)
    mix = _mm(merged, w_o[0], col0=0, n_cols=D_MODEL, tn=TN, out_dtype=F32, name="out_proj")

    x1, h2 = _ln1(xp, xs, pos, mix, mods3, ln1_g, ln1_b)
    u = _mm(h2, w_up[0], col0=0, n_cols=D_FF, tn=TN, out_dtype=BF16, relu2=True, name="up_proj")
    ffn = _down(u, w_down[0])

    y_ctx = _ln2(x1, ffn, mods3, ln2_g, ln2_b, row0=0, n_rows=M_CTX)
    y_lat = _ln2(x1, ffn, mods3, ln2_g, ln2_b, row0=M_CTX, n_rows=M_LAT)

    return (y_ctx.reshape(x_prompt.shape), y_lat.reshape(x_sample.shape),
            lru_state.reshape(N_CTX_SEQ, 1, 2, D_LRU),
            dn_state.reshape(N_CTX_SEQ, 1, 2, HEADS, HEAD_D, HEAD_D))
```

```python
import functools
import math

import jax
import jax.numpy as jnp
import numpy as np
from jax import lax
from jax.experimental import pallas as pl
from jax.experimental.pallas import tpu as pltpu

F32 = jnp.float32
BF16 = jnp.bfloat16

D_MODEL = 4096
N_CTX_SEQ, CTX_T = 16, 256
N_LAT_SEQ, LAT_T = 2, 1024
M_CTX = N_CTX_SEQ * CTX_T
M_LAT = N_LAT_SEQ * LAT_T
M_TOK = M_CTX + M_LAT
GRID_W = 64
D_LRU = 2048
LRU_C = 8.0
HEADS = 16
HEAD_D = 128
DN_QK = HEADS * HEAD_D
CHUNK = 64
SUPER = 256
DN_HEADS_PER_STEP_CTX = 4
DN_HEADS_PER_STEP_LAT = 4
D_FF = 4 * D_MODEL
ALPHA = 2.0 ** 0.25
LN_EPS = 1e-5
RMS_EPS = 1e-6
POS_BASE = 10000.0
COL_LRU_X, COL_LRU_Y, COL_Q, COL_K, COL_V, COL_Z = 0, 2048, 4096, 6144, 8192, 10240
COL_BD = 12288
COL_GATE = 12352
N_MAIN = COL_BD
VMEM_LIMIT = 56 * 1024 * 1024
TM = 1024
TN = 512


def _cparams(n_axes):
    return pltpu.CompilerParams(dimension_semantics=("arbitrary",) * n_axes, vmem_limit_bytes=VMEM_LIMIT)


def _row_group(i, tm):
    n_ctx = M_CTX // tm
    return jnp.where(i < n_ctx, 0, 1 + (i - n_ctx) // (LAT_T // tm))


def _mod_spec(tm, k):
    return pl.BlockSpec((1, 1, D_MODEL), lambda i: (_row_group(i, tm), 0, k))


def _sigmoid(x):
    return 0.5 * (1.0 + jnp.tanh(0.5 * x))


def _silu(x):
    return x * _sigmoid(x)


def _softplus(x):
    return jnp.maximum(x, 0.0) + jnp.log1p(jnp.exp(-jnp.abs(x)))


def _dot(a, b):
    return jnp.dot(a, b, preferred_element_type=F32)


def _mod_kernel(c_ref, w_ref, b_ref, o_ref):
    s = _silu(c_ref[...]).astype(BF16)
    o_ref[...] = _dot(s, w_ref[...].astype(BF16)) + b_ref[...]


def _modulation(cc, w_mod, b_mod):
    tn = 512
    n = w_mod.shape[1]
    return pl.pallas_call(
        _mod_kernel,
        grid=(n // tn,),
        in_specs=[pl.BlockSpec((8, D_MODEL), lambda j: (0, 0)),
                  pl.BlockSpec((D_MODEL, tn), lambda j: (0, j)),
                  pl.BlockSpec((1, tn), lambda j: (0, j))],
        out_specs=pl.BlockSpec((8, tn), lambda j: (0, j)),
        out_shape=jax.ShapeDtypeStruct((8, n), F32),
        compiler_params=_cparams(1),
        name="modulation",
    )(cc, w_mod, b_mod)


def _token_specs(tm):
    n_ctx = M_CTX // tm
    n_pos = LAT_T // tm
    return [pl.BlockSpec((tm, D_MODEL), lambda i: (jnp.minimum(i, n_ctx - 1), 0)),
            pl.BlockSpec((tm, D_MODEL), lambda i: (jnp.maximum(i - n_ctx, 0), 0)),
            pl.BlockSpec((tm, D_MODEL), lambda i: (jnp.maximum(i - n_ctx, 0) % n_pos, 0))]


def _prep_kernel(xp_ref, xs_ref, pos_ref, sh_ref, sc_ref, wbd_ref, h_ref, bd_ref, wbd_b_ref, *, n_ctx_tiles):
    i = pl.program_id(0)
    sc = 1.0 + sc_ref[0]
    sh = sh_ref[0]

    @pl.when(i == 0)
    def _():
        wbd_b_ref[...] = wbd_ref[...].T.astype(BF16)

    def finish(x):
        h = (x * sc + sh).astype(BF16)
        h_ref[...] = h
        bd_ref[...] = _dot(h, wbd_b_ref[...])

    @pl.when(i < n_ctx_tiles)
    def _():
        finish(xp_ref[...])

    @pl.when(i >= n_ctx_tiles)
    def _():
        finish(xs_ref[...] + pos_ref[...])


def _prep(xp, xs, pos, mods3, w_in_t):
    tm = 256
    return pl.pallas_call(
        functools.partial(_prep_kernel, n_ctx_tiles=M_CTX // tm),
        grid=(M_TOK // tm,),
        in_specs=_token_specs(tm) + [_mod_spec(tm, 0), _mod_spec(tm, 1),
                                     pl.BlockSpec((pl.Element(HEAD_D), pl.Element(D_MODEL)), lambda i: (COL_BD, 0))],
        out_specs=[pl.BlockSpec((tm, D_MODEL), lambda i: (i, 0)), pl.BlockSpec((tm, HEAD_D), lambda i: (i, 0))],
        out_shape=[jax.ShapeDtypeStruct((M_TOK, D_MODEL), BF16), jax.ShapeDtypeStruct((M_TOK, HEAD_D), F32)],
        scratch_shapes=[pltpu.VMEM((D_MODEL, HEAD_D), BF16)],
        compiler_params=_cparams(1),
        name="prep",
    )(xp, xs, pos, mods3, mods3, w_in_t)


LHS_SLOTS = 3


def _mm_kernel(lhs_hbm, w_ref, *rest, relu2, sigmoid_bias, w_transposed, kc, tm, n_i, n_steps):
    if sigmoid_bias:
        bias_ref, o_ref, wb_ref, lbuf, sem = rest
    else:
        o_ref, wb_ref, lbuf, sem = rest
    k_dim = lbuf.shape[2]
    i = pl.program_id(1)
    step = pl.program_id(0) * n_i + i

    def lhs_copy(s, slot):
        rows = pl.ds(pl.multiple_of((s % n_i) * tm, tm), tm)
        return pltpu.make_async_copy(lhs_hbm.at[rows, :], lbuf.at[slot], sem.at[slot])

    @pl.when(step == 0)
    def _():
        lhs_copy(0, 0).start()
        lhs_copy(1, 1).start()

    @pl.when(step + 2 < n_steps)
    def _():
        lhs_copy(step + 2, (step + 2) % LHS_SLOTS).start()

    slot = step % LHS_SLOTS
    lhs_copy(step, slot).wait()

    def finish(acc):
        if relu2:
            acc = jnp.maximum(acc, 0.0)
            acc = acc * acc
        if sigmoid_bias:
            acc = _sigmoid(acc + bias_ref[...])
        o_ref[...] = acc.astype(o_ref.dtype)

    @pl.when(i == 0)
    def _():
        acc = None
        for c in range(k_dim // kc):
            ks = slice(c * kc, (c + 1) * kc)
            wb = (w_ref[:, ks].T if w_transposed else w_ref[ks, :]).astype(BF16)
            wb_ref[ks, :] = wb
            part = _dot(lbuf[slot, :, ks], wb)
            acc = part if acc is None else acc + part
        finish(acc)

    @pl.when(i > 0)
    def _():
        finish(_dot(lbuf[slot], wb_ref[...]))


def _mm(lhs, w, *, col0, n_cols, tn, out_dtype, relu2=False, sigmoid_bias=None, w_transposed=False, name):
    m, k = lhs.shape
    tm = TM
    n_j, n_i = n_cols // tn, m // tm
    if w_transposed:
        assert col0 % 8 == 0 and tn % 8 == 0
        w_spec = pl.BlockSpec((pl.Element(tn), pl.Element(k)), lambda j, i: (pl.multiple_of(col0 + j * tn, 8), 0))
    else:
        w_spec = pl.BlockSpec((k, tn), lambda j, i: (0, col0 // tn + j))
    in_specs = [pl.BlockSpec(memory_space=pl.ANY), w_spec]
    args = [lhs, w]
    if sigmoid_bias is not None:
        in_specs.append(pl.BlockSpec((1, tn), lambda j, i: (0, j)))
        args.append(sigmoid_bias)
    assert n_j * n_i >= LHS_SLOTS
    return pl.pallas_call(
        functools.partial(_mm_kernel, relu2=relu2, sigmoid_bias=sigmoid_bias is not None,
                          w_transposed=w_transposed, kc=512, tm=tm, n_i=n_i, n_steps=n_j * n_i),
        grid=(n_j, n_i),
        in_specs=in_specs,
        out_specs=pl.BlockSpec((tm, tn), lambda j, i: (i, j)),
        out_shape=jax.ShapeDtypeStruct((m, n_cols), out_dtype),
        scratch_shapes=[pltpu.VMEM((k, tn), BF16), pltpu.VMEM((LHS_SLOTS, tm, k), BF16),
                        pltpu.SemaphoreType.DMA((LHS_SLOTS,))],
        compiler_params=_cparams(2),
        name=name,
    )(*args)


def _down_kernel(lhs_ref, w_ref, o_ref, wb_ref, *, tm, kc):
    k = pl.program_id(1)
    i = pl.program_id(2)
    rows = pl.ds(pl.multiple_of(i * tm, tm), tm)
    tk = lhs_ref.shape[1]

    def accumulate(acc):
        @pl.when(k == 0)
        def _():
            o_ref[rows, :] = acc

        @pl.when(k > 0)
        def _():
            o_ref[rows, :] += acc

    @pl.when(i == 0)
    def _():
        acc = None
        for c in range(tk // kc):
            ks = slice(c * kc, (c + 1) * kc)
            wb = w_ref[ks, :].astype(BF16)
            wb_ref[ks, :] = wb
            part = _dot(lhs_ref[:, ks], wb)
            acc = part if acc is None else acc + part
        accumulate(acc)

    @pl.when(i > 0)
    def _():
        accumulate(_dot(lhs_ref[...], wb_ref[...]))


def _down(u, w_down):
    tk, tm = 4096, 1024
    m = u.shape[0]
    return pl.pallas_call(
        functools.partial(_down_kernel, tm=tm, kc=512),
        grid=(D_MODEL // TN, D_FF // tk, m // tm),
        in_specs=[pl.BlockSpec((tm, tk), lambda j, k, i: (i, k)),
                  pl.BlockSpec((tk, TN), lambda j, k, i: (k, j))],
        out_specs=pl.BlockSpec((m, TN), lambda j, k, i: (0, j), pipeline_mode=pl.Buffered(1)),
        out_shape=jax.ShapeDtypeStruct((m, D_MODEL), F32),
        scratch_shapes=[pltpu.VMEM((tk, TN), BF16)],
        compiler_params=_cparams(3),
        name="down_proj",
    )(u, w_down)


def _conv4(x, w):
    t = x.shape[0]
    row = lax.broadcasted_iota(jnp.int32, x.shape, 0)
    xm2 = jnp.where(row >= 2, pltpu.roll(x, 2, 0), 0.0)
    xm1 = jnp.where(row >= 1, pltpu.roll(x, 1, 0), 0.0)
    xp1 = jnp.where(row < t - 1, pltpu.roll(x, t - 1, 0), 0.0)
    return w[0:1] * xm2 + w[1:2] * xm1 + w[2:3] * x + w[3:4] * xp1


def _lru_kernel(x_ref, y_ref, cw_ref, cb_ref, gw_ref, gb_ref, lam_ref, h0_ref, o_ref, st_ref,
                af_ref, bf_ref, ab_ref, bb_ref, *, t_len, width):
    for n in range(width // HEAD_D):
        sl = slice(n * HEAD_D, (n + 1) * HEAD_D)
        xc = _conv4(x_ref[:, sl], cw_ref[:, sl]) + cb_ref[:, sl]
        xcb = xc.astype(BF16)
        for d, (a_ref, b_ref) in enumerate(((af_ref, bf_ref), (ab_ref, bb_ref))):
            pre_r = _dot(xcb, gw_ref[d, 0, n].astype(BF16))
            pre_i = _dot(xcb, gw_ref[d, 1, n].astype(BF16))
            r = _sigmoid(pre_r + gb_ref[2 * d:2 * d + 1, sl])
            ig = _sigmoid(pre_i + gb_ref[2 * d + 1:2 * d + 2, sl])
            log_a = (-LRU_C) * r * _softplus(-lam_ref[d:d + 1, sl])
            a = jnp.exp(log_a)
            mult = jnp.sqrt((1.0 + a * a) * jnp.tanh(-log_a))
            a_ref[:, sl] = a
            b_ref[:, sl] = mult * (ig * xc)

    n_tiles = t_len // 8
    rowi = lax.broadcasted_iota(jnp.int32, (8, width), 0)

    def body(g, carry):
        cf, cb = carry
        r0 = pl.multiple_of(g * 8, 8)
        a8 = af_ref[pl.ds(r0, 8), :]
        b8 = bf_ref[pl.ds(r0, 8), :]
        for dd in (1, 2, 4):
            m = rowi >= dd
            a_sh = jnp.where(m, pltpu.roll(a8, dd, 0), 1.0)
            b_sh = jnp.where(m, pltpu.roll(b8, dd, 0), 0.0)
            b8 = a8 * b_sh + b8
            a8 = a8 * a_sh
        h8 = a8 * cf + b8
        bf_ref[pl.ds(r0, 8), :] = h8
        cf = h8[7:8, :]

        r1 = pl.multiple_of((n_tiles - 1 - g) * 8, 8)
        a8 = ab_ref[pl.ds(r1, 8), :]
        b8 = bb_ref[pl.ds(r1, 8), :]
        for dd in (1, 2, 4):
            m = rowi < 8 - dd
            a_sh = jnp.where(m, pltpu.roll(a8, 8 - dd, 0), 1.0)
            b_sh = jnp.where(m, pltpu.roll(b8, 8 - dd, 0), 0.0)
            b8 = a8 * b_sh + b8
            a8 = a8 * a_sh
        h8 = a8 * cb + b8
        bb_ref[pl.ds(r1, 8), :] = h8
        cb = h8[0:1, :]
        return cf, cb

    cf, cb = lax.fori_loop(0, n_tiles, body, (h0_ref[0, 0:1, :], h0_ref[0, 1:2, :]))
    st_ref[0, 0:1, :] = cf
    st_ref[0, 1:2, :] = cb
    y = y_ref[...]
    gelu = 0.5 * y * (1.0 + jnp.tanh(math.sqrt(2.0 / math.pi) * (y + 0.044715 * (y * y * y))))
    o_ref[...] = ((bf_ref[...] + bb_ref[...]) * gelu).astype(BF16)


def _lru(p_main, conv_w, conv_b, gate_w, gate_b4, lam, h0, *, t_len, row_block0, n_seq):
    width = 512
    nblk = width // HEAD_D
    ncb = D_LRU // width
    return pl.pallas_call(
        functools.partial(_lru_kernel, t_len=t_len, width=width),
        grid=(n_seq, ncb),
        in_specs=[pl.BlockSpec((t_len, width), lambda b, c: (b + row_block0, c)),
                  pl.BlockSpec((t_len, width), lambda b, c: (b + row_block0, c + COL_LRU_Y // width)),
                  pl.BlockSpec((4, width), lambda b, c: (0, c)),
                  pl.BlockSpec((1, width), lambda b, c: (0, c)),
                  pl.BlockSpec((2, 2, nblk, HEAD_D, HEAD_D), lambda b, c: (0, 0, c, 0, 0)),
                  pl.BlockSpec((4, width), lambda b, c: (0, c)),
                  pl.BlockSpec((2, width), lambda b, c: (0, c)),
                  pl.BlockSpec((1, 2, width), lambda b, c: (b, 0, c))],
        out_specs=[pl.BlockSpec((t_len, width), lambda b, c: (b, c)),
                   pl.BlockSpec((1, 2, width), lambda b, c: (b, 0, c))],
        out_shape=[jax.ShapeDtypeStruct((n_seq * t_len, D_LRU), BF16),
                   jax.ShapeDtypeStruct((n_seq, 2, D_LRU), F32)],
        scratch_shapes=[pltpu.VMEM((t_len, width), F32)] * 4,
        compiler_params=_cparams(2),
        name=f"rg_lru_t{t_len}",
    )(p_main, p_main, conv_w, conv_b, gate_w, gate_b4, lam, h0)


N_LEVELS = 6
MASKB_INCL, MASKB_LEVEL0, MASKB_EYE = 0, 1, 1 + N_LEVELS


def _dn_masks():
    i = np.arange(SUPER)[:, None]
    j = np.arange(SUPER)[None, :]
    same = (i // CHUNK) == (j // CHUNK)
    out = np.zeros((2, MASKB_EYE + 1, SUPER, SUPER), np.float32)
    for d in range(2):
        before = (j > i) if d == 1 else (j < i)
        out[d, MASKB_INCL] = same & (before | (i == j))
        s = 1
        for lv in range(N_LEVELS):
            blk = (i // (2 * s)) == (j // (2 * s))
            il, jl = i % (2 * s), j % (2 * s)
            out[d, MASKB_LEVEL0 + lv] = blk & (((il < s) & (jl >= s)) if d == 1 else ((il >= s) & (jl < s)))
            s *= 2
        out[d, MASKB_EYE] = (i == j)
    return out


def _dn_block(chains, g_rows, m_ref, mb_ref, s_ref, vn_ref):
    nt = (((1,), (1,)), ((), ()))
    tn = (((0,), (0,)), ((), ()))
    n_chunks = SUPER // CHUNK
    gam_all = []
    for d in range(2):
        incl_b = mb_ref[d, MASKB_INCL]
        g_hi = g_rows[d].astype(BF16)
        r = g_rows[d] - g_hi.astype(F32)
        g_mid = r.astype(BF16)
        g_lo = (r - g_mid.astype(F32)).astype(BF16)
        gam_all.append(_dot(incl_b, g_hi) + (_dot(incl_b, g_mid) + _dot(incl_b, g_lo)))
    lane = lax.broadcasted_iota(jnp.int32, (SUPER, HEAD_D), 1)
    st = []
    for q, k, v, beta_b, g_lane, d, idx in chains:
        gam_col = jnp.sum(jnp.where(lane == g_lane, gam_all[d], 0.0), axis=1, keepdims=True)
        gam1 = jnp.broadcast_to(gam_col, (SUPER, HEAD_D))
        kb = k * beta_b
        kbf = k.astype(BF16)
        kk = lax.dot_general(kb.astype(BF16), kbf, nt, preferred_element_type=F32)
        qk = lax.dot_general(q.astype(BF16), kbf, nt, preferred_element_type=F32)
        st.append(dict(q=q, k=k, v=v, beta_b=beta_b, d=d, idx=idx, gam1=gam1, gam_col=gam_col, kb=kb, kk=kk, qk=qk))
    for c in st:
        d = c["d"]
        gam = jnp.broadcast_to(c.pop("gam_col"), (SUPER, SUPER))
        decay = jnp.exp(jnp.minimum(gam - gam.T, 0.0)) * m_ref[d]
        c["lmat"] = (c.pop("kk") * decay).astype(BF16)
        c["attn"] = (c.pop("qk") * decay).astype(BF16)
        c["minv"] = mb_ref[d, MASKB_EYE] - c["lmat"] * mb_ref[d, MASKB_LEVEL0]

    for lv in range(1, N_LEVELS):
        for c in st:
            c["x"] = _dot(c["lmat"] * mb_ref[c["d"], MASKB_LEVEL0 + lv], c["minv"]).astype(BF16)
        for c in st:
            c["minv"] = c["minv"] - _dot(c["minv"], c.pop("x")).astype(BF16)
    for c in st:
        eg = jnp.exp(c["gam1"])
        rhs = jnp.concatenate([c["v"] * c["beta_b"], c["kb"] * eg], axis=1)
        uw = _dot(c.pop("minv"), rhs.astype(BF16))
        c["u"], c["w"] = uw[:, :HEAD_D], uw[:, HEAD_D:].astype(BF16)
        c["q_dec"] = (c["q"] * eg).astype(BF16)
        lasts = []
        for n in range(n_chunks):
            r = n * CHUNK if c["d"] == 1 else n * CHUNK + CHUNK - 1
            lasts.append(jnp.broadcast_to(c["gam1"][r:r + 1, :], (CHUNK, HEAD_D)))
        c["g_last"] = jnp.concatenate(lasts, axis=0)
        c["k_dec"] = (c["k"] * jnp.exp(c["g_last"] - c["gam1"])).astype(BF16)
        c["outs"] = [None] * n_chunks
        vn_ref[c["idx"]] = jnp.zeros((SUPER, HEAD_D), F32)

    for step in range(n_chunks):
        for c in st:
            n = n_chunks - 1 - step if c["d"] == 1 else step
            rows = slice(n * CHUNK, (n + 1) * CHUNK)
            c["s"] = s_ref[c["idx"]]
            c["sb"] = c["s"].astype(BF16)
            c["v_new"] = c["u"][rows] - _dot(c["w"][rows], c["sb"])
            vn_ref[c["idx"], rows, :] = c["v_new"]
        for c in st:
            n = n_chunks - 1 - step if c["d"] == 1 else step
            rows = slice(n * CHUNK, (n + 1) * CHUNK)
            c["outs"][n] = (_dot(c["q_dec"][rows], c["sb"])
                            + _dot(c["attn"][rows], vn_ref[c["idx"]].astype(BF16)))
            inc = lax.dot_general(c["k_dec"][rows], c["v_new"].astype(BF16), tn, preferred_element_type=F32)
            s_ref[c["idx"]] = c["s"] * jnp.exp(c["g_last"][n * CHUNK:n * CHUNK + 1, :]) + inc
    return [jnp.concatenate(c["outs"], axis=0) for c in st]


def _gdn_kernel(*refs, t_len, use_state, emit_state, hps):
    (q_ref, k_ref, v_ref, z_ref, bd_ref, cwq_ref, cwk_ref, cwv_ref, alog_ref, dt_ref, nw_ref,
     m_ref, mb_ref), refs = refs[:13], refs[13:]
    if use_state:
        s0_ref, refs = refs[0], refs[1:]
    o_ref, refs = refs[0], refs[1:]
    if emit_state:
        st_ref, refs = refs[0], refs[1:]
    qs_ref, ks_ref, vs_ref, bet_ref, gall_ref, acc_ref, s_ref, vn_ref = refs

    head0 = pl.program_id(1) * hps

    def l2n(x):
        return x * lax.rsqrt(jnp.sum(x * x, axis=-1, keepdims=True) + RMS_EPS)

    bd = bd_ref[...]
    beta_all = _sigmoid(bd)
    gall_ref[...] = -jnp.exp(alog_ref[...]) * _softplus(bd + dt_ref[...])
    lane = lax.broadcasted_iota(jnp.int32, bd.shape, 1)
    for hh in range(hps):
        sl = slice(hh * HEAD_D, (hh + 1) * HEAD_D)
        qs_ref[:, sl] = l2n(_silu(_conv4(q_ref[:, sl], cwq_ref[:, sl]))) * (HEAD_D ** -0.5)
        ks_ref[:, sl] = l2n(_silu(_conv4(k_ref[:, sl], cwk_ref[:, sl])))
        vs_ref[:, sl] = _silu(_conv4(v_ref[:, sl], cwv_ref[:, sl]))
        for d in range(2):
            col = d * HEADS + head0 + hh
            beta = jnp.sum(jnp.where(lane == col, beta_all, 0.0), axis=1, keepdims=True)
            bet_ref[d, :, sl] = jnp.broadcast_to(beta, (t_len, HEAD_D))
            if use_state:
                s_ref[2 * hh + d] = s0_ref[0, d, hh]
            else:
                s_ref[2 * hh + d] = jnp.zeros((HEAD_D, HEAD_D), F32)

    n_blocks = t_len // SUPER

    def block_step(nb):
        chains, dests, g_rows = [], [], []
        for d in range(2):
            blk = nb if d == 0 else n_blocks - 1 - nb
            g_rows.append(gall_ref[pl.ds(pl.multiple_of(blk * SUPER, SUPER), SUPER), :])
        for hh in range(hps):
            sl = slice(hh * HEAD_D, (hh + 1) * HEAD_D)
            for d in range(2):
                blk = nb if d == 0 else n_blocks - 1 - nb
                rows = pl.ds(pl.multiple_of(blk * SUPER, SUPER), SUPER)
                chains.append((qs_ref[rows, sl], ks_ref[rows, sl], vs_ref[rows, sl], bet_ref[d, rows, sl],
                               2 * HEADS + d * HEADS + head0 + hh, d, 2 * hh + d))
                dests.append((d, rows, sl))
        for (d, rows, sl), out in zip(dests, _dn_block(chains, g_rows, m_ref, mb_ref, s_ref, vn_ref)):
            acc_ref[d, rows, sl] = out

    if n_blocks == 1:
        block_step(0)
    else:
        def loop_body(nb, carry):
            block_step(nb)
            return carry
        lax.fori_loop(0, n_blocks, loop_body, 0)

    for hh in range(hps):
        sl = slice(hh * HEAD_D, (hh + 1) * HEAD_D)
        if emit_state:
            for d in range(2):
                st_ref[0, d, hh] = s_ref[2 * hh + d]
        o = acc_ref[0, :, sl] + acc_ref[1, :, sl]
        o = o * lax.rsqrt(jnp.mean(o * o, axis=-1, keepdims=True) + RMS_EPS) * nw_ref[...]
        o_ref[:, sl] = (o * _silu(z_ref[:, sl])).astype(BF16)


def _gdn(p_main, p_bd, conv_w, alog_vec, dt_vec, norm_w, mask_incl, masks_b, s0, *,
         t_len, row_block0, n_seq, emit_state, hps):
    use_state = s0 is not None
    wid = hps * HEAD_D
    cb = lambda col: col // wid
    buffering = dict(pipeline_mode=pl.Buffered(1)) if t_len > CTX_T else {}
    tok = lambda col: pl.BlockSpec((t_len, wid), lambda b, h: (b + row_block0, h + cb(col)), **buffering)
    cws = lambda col: pl.BlockSpec((4, wid), lambda b, h: (0, h + cb(col - COL_Q)))
    vec = pl.BlockSpec((1, HEAD_D), lambda b, h: (0, 0))
    state_spec = pl.BlockSpec((1, 2, hps, HEAD_D, HEAD_D), lambda b, h: (b, 0, h, 0, 0))
    in_specs = [tok(COL_Q), tok(COL_K), tok(COL_V), tok(COL_Z),
                pl.BlockSpec((t_len, HEAD_D), lambda b, h: (b + row_block0, 0)),
                cws(COL_Q), cws(COL_K), cws(COL_V), vec, vec, vec,
                pl.BlockSpec(mask_incl.shape, lambda b, h: (0, 0, 0)),
                pl.BlockSpec(masks_b.shape, lambda b, h: (0, 0, 0, 0))]
    args = [p_main, p_main, p_main, p_main, p_bd, conv_w, conv_w, conv_w, alog_vec, dt_vec, norm_w,
            mask_incl, masks_b]
    if use_state:
        in_specs.append(state_spec)
        args.append(s0)
    out_specs = [pl.BlockSpec((t_len, wid), lambda b, h: (b, h))]
    out_shape = [jax.ShapeDtypeStruct((n_seq * t_len, DN_QK), BF16)]
    if emit_state:
        out_specs.append(state_spec)
        out_shape.append(jax.ShapeDtypeStruct((n_seq, 2, HEADS, HEAD_D, HEAD_D), F32))
    scratch = ([pltpu.VMEM((t_len, wid), F32)] * 3
               + [pltpu.VMEM((2, t_len, wid), F32),
                  pltpu.VMEM((t_len, HEAD_D), F32),
                  pltpu.VMEM((2, t_len, wid), F32)]
               + [pltpu.VMEM((2 * hps, HEAD_D, HEAD_D), F32),
                  pltpu.VMEM((2 * hps, SUPER, HEAD_D), F32)])
    return pl.pallas_call(
        functools.partial(_gdn_kernel, t_len=t_len, use_state=use_state, emit_state=emit_state, hps=hps),
        grid=(n_seq, HEADS // hps),
        in_specs=in_specs, out_specs=out_specs, out_shape=out_shape,
        scratch_shapes=scratch,
        compiler_params=_cparams(2),
        name=f"gated_delta_t{t_len}",
    )(*args)


def _merge_kernel(ac_ref, al_ref, bc_ref, bl_ref, wl_ref, wd_ref, g0_ref, g1_ref, o_ref,
                  wlb_ref, wdb_ref, *, n_ctx_tiles, kc):
    i = pl.program_id(1)

    def finish(pl_, pd_):
        o_ref[...] = (g0_ref[...].astype(F32) * pl_ + g1_ref[...].astype(F32) * pd_).astype(BF16)

    @pl.when(i == 0)
    def _():
        accs = []
        for a_ref, w_ref, wb_ref in ((ac_ref, wl_ref, wlb_ref), (bc_ref, wd_ref, wdb_ref)):
            acc = None
            for c in range(w_ref.shape[0] // kc):
                ks = slice(c * kc, (c + 1) * kc)
                wb = w_ref[ks, :].astype(BF16)
                wb_ref[ks, :] = wb
                part = _dot(a_ref[:, ks], wb)
                acc = part if acc is None else acc + part
            accs.append(acc)
        finish(*accs)

    @pl.when((i > 0) & (i < n_ctx_tiles))
    def _():
        finish(_dot(ac_ref[...], wlb_ref[...]), _dot(bc_ref[...], wdb_ref[...]))

    @pl.when(i >= n_ctx_tiles)
    def _():
        finish(_dot(al_ref[...], wlb_ref[...]), _dot(bl_ref[...], wdb_ref[...]))


def _merge(lru_ctx, lru_lat, dn_ctx, dn_lat, w_lru, w_dn, gates):
    tm = 1024
    nj = D_MODEL // TN
    n_ctx = M_CTX // tm
    ctx = pl.BlockSpec((tm, D_LRU), lambda j, i: (jnp.minimum(i, n_ctx - 1), 0))
    lat = pl.BlockSpec((tm, D_LRU), lambda j, i: (jnp.maximum(i - n_ctx, 0), 0))
    wspec = pl.BlockSpec((D_LRU, TN), lambda j, i: (0, j), pipeline_mode=pl.Buffered(1))
    return pl.pallas_call(
        functools.partial(_merge_kernel, n_ctx_tiles=n_ctx, kc=512),
        grid=(nj, M_TOK // tm),
        in_specs=[ctx, lat, ctx, lat, wspec, wspec,
                  pl.BlockSpec((tm, TN), lambda j, i: (i, j)),
                  pl.BlockSpec((tm, TN), lambda j, i: (i, j + nj))],
        out_specs=pl.BlockSpec((tm, TN), lambda j, i: (i, j)),
        out_shape=jax.ShapeDtypeStruct((M_TOK, D_MODEL), BF16),
        scratch_shapes=[pltpu.VMEM((D_LRU, TN), BF16), pltpu.VMEM((DN_QK, TN), BF16)],
        compiler_params=_cparams(2),
        name="branch_merge",
    )(lru_ctx, lru_lat, dn_ctx, dn_lat, w_lru, w_dn, gates, gates)


def _layer_norm(x, g, b):
    mu = jnp.mean(x, axis=-1, keepdims=True)
    xc = x - mu
    var = jnp.mean(xc * xc, axis=-1, keepdims=True)
    return xc * lax.rsqrt(var + LN_EPS) * g + b


def _ln1_kernel(xp_ref, xs_ref, pos_ref, mix_ref, gm_ref, shf_ref, scf_ref, g_ref, b_ref, x1_ref, h2_ref,
                *, n_ctx_tiles):
    i = pl.program_id(0)

    def finish(x):
        x1 = _layer_norm(ALPHA * x + gm_ref[0] * mix_ref[...], g_ref[...], b_ref[...])
        x1_ref[...] = x1
        h2_ref[...] = (x1 * (1.0 + scf_ref[0]) + shf_ref[0]).astype(BF16)

    @pl.when(i < n_ctx_tiles)
    def _():
        finish(xp_ref[...])

    @pl.when(i >= n_ctx_tiles)
    def _():
        finish(xs_ref[...] + pos_ref[...])


def _ln1(xp, xs, pos, mix, mods3, g, b):
    tm = 256
    row = pl.BlockSpec((tm, D_MODEL), lambda i: (i, 0))
    vec = pl.BlockSpec((1, D_MODEL), lambda i: (0, 0))
    return pl.pallas_call(
        functools.partial(_ln1_kernel, n_ctx_tiles=M_CTX // tm),
        grid=(M_TOK // tm,),
        in_specs=_token_specs(tm) + [row, _mod_spec(tm, 2), _mod_spec(tm, 3), _mod_spec(tm, 4), vec, vec],
        out_specs=[row, row],
        out_shape=[jax.ShapeDtypeStruct((M_TOK, D_MODEL), F32), jax.ShapeDtypeStruct((M_TOK, D_MODEL), BF16)],
        compiler_params=_cparams(1),
        name="residual_ln1",
    )(xp, xs, pos, mix, mods3, mods3, mods3, g, b)


def _ln2_kernel(x1_ref, ffn_ref, gf_ref, g_ref, b_ref, y_ref):
    y_ref[...] = _layer_norm(ALPHA * x1_ref[...] + gf_ref[0] * ffn_ref[...], g_ref[...], b_ref[...])


def _ln2(x1, ffn, mods3, g, b, *, row0, n_rows):
    tm = 256
    t0 = row0 // tm
    row = pl.BlockSpec((tm, D_MODEL), lambda i: (i + t0, 0))
    vec = pl.BlockSpec((1, D_MODEL), lambda i: (0, 0))
    return pl.pallas_call(
        _ln2_kernel,
        grid=(n_rows // tm,),
        in_specs=[row, row, pl.BlockSpec((1, 1, D_MODEL), lambda i: (_row_group(i + t0, tm), 0, 5)), vec, vec],
        out_specs=pl.BlockSpec((tm, D_MODEL), lambda i: (i, 0)),
        out_shape=jax.ShapeDtypeStruct((n_rows, D_MODEL), F32),
        compiler_params=_cparams(1),
        name=f"residual_ln2_r{row0}",
    )(x1, ffn, mods3, g, b)


def _grid_pos_embed():
    t = np.arange(LAT_T)
    quarter = D_MODEL // 4
    omega = 1.0 / (POS_BASE ** (np.arange(quarter, dtype=np.float64) / quarter))
    er = (t // GRID_W)[:, None] * omega
    ec = (t % GRID_W)[:, None] * omega
    return jnp.asarray(np.concatenate([np.sin(er), np.cos(er), np.sin(ec), np.cos(ec)], axis=-1), dtype=F32)


def _lane_vec(v2x16):
    return jnp.zeros((1, HEAD_D), F32).at[0, 2 * HEADS:4 * HEADS].set(v2x16.reshape(-1))


def kernel(x_prompt, x_sample, state_lru, state_dn, c, c_ctx, w_mod, b_mod, w_in, lru_conv_w, lru_conv_b, lru_gate_w, lru_gate_b, lru_lambda, dn_conv_w, dn_a_log, dn_dt_bias, dn_norm_w, b_branch, w_lru_proj, w_dn_proj, w_o, ln1_g, ln1_b, w_up, w_down, ln2_g, ln2_b):
    xp = x_prompt.reshape(M_CTX, D_MODEL)
    xs = x_sample.reshape(M_LAT, D_MODEL)
    pos = _grid_pos_embed()

    cc = jnp.zeros((8, D_MODEL), F32).at[0].set(c_ctx).at[1:1 + N_LAT_SEQ].set(c)
    mods = _modulation(cc, w_mod[0], b_mod)
    mods3 = mods.reshape(8, 1, 6 * D_MODEL)

    w_in_t = jnp.swapaxes(w_in, 1, 2)[0]
    h, p_bd = _prep(xp, xs, pos, mods3, w_in_t)
    p_main = _mm(h, w_in_t, col0=0, n_cols=N_MAIN, tn=TN, out_dtype=F32, w_transposed=True, name="in_proj_main")
    gates = _mm(h, w_in_t, col0=COL_GATE, n_cols=2 * D_MODEL, tn=TN, out_dtype=BF16, w_transposed=True,
                sigmoid_bias=b_branch.reshape(1, 2 * D_MODEL), name="in_proj_gates")

    gate_b4 = lru_gate_b[0].reshape(4, D_LRU)
    lru_args = (lru_conv_w[0], lru_conv_b, lru_gate_w[0], gate_b4, lru_lambda[0])
    lru_ctx, lru_state = _lru(p_main, *lru_args, jnp.zeros((N_CTX_SEQ, 2, D_LRU), F32),
                              t_len=CTX_T, row_block0=0, n_seq=N_CTX_SEQ)
    lru_lat, _ = _lru(p_main, *lru_args, state_lru[:, 0],
                      t_len=LAT_T, row_block0=M_CTX // LAT_T, n_seq=N_LAT_SEQ)

    masks = _dn_masks()
    dn_args = (dn_conv_w[0], _lane_vec(dn_a_log[0]), _lane_vec(dn_dt_bias[0]), dn_norm_w,
               jnp.asarray(masks[:, MASKB_INCL]), jnp.asarray(masks, dtype=BF16))
    dn_ctx, dn_state = _gdn(p_main, p_bd, *dn_args, None,
                            t_len=CTX_T, row_block0=0, n_seq=N_CTX_SEQ, emit_state=True, hps=DN_HEADS_PER_STEP_CTX)
    (dn_lat,) = _gdn(p_main, p_bd, *dn_args, state_dn[:, 0],
                     t_len=LAT_T, row_block0=M_CTX // LAT_T, n_seq=N_LAT_SEQ, emit_state=False, hps=DN_HEADS_PER_STEP_LAT)

    merged = _merge(lru_ctx, lru_lat, dn_ctx, dn_lat, w_lru_proj[0], w_dn_proj[0], gates)
    mix = _mm(merged, w_o[0], col0=0, n_cols=D_MODEL, tn=TN, out_dtype=F32, name="out_proj")

    x1, h2 = _ln1(xp, xs, pos, mix, mods3, ln1_g, ln1_b)
    u = _mm(h2, w_up[0], col0=0, n_cols=D_FF, tn=TN, out_dtype=BF16, relu2=True, name="up_proj")
    ffn = _down(u, w_down[0])

    y_ctx = _ln2(x1, ffn, mods3, ln2_g, ln2_b, row0=0, n_rows=M_CTX)
    y_lat = _ln2(x1, ffn, mods3, ln2_g, ln2_b, row0=M_CTX, n_rows=M_LAT)

    return (y_ctx.reshape(x_prompt.shape), y_lat.reshape(x_sample.shape),
            lru_state.reshape(N_CTX_SEQ, 1, 2, D_LRU),
            dn_state.reshape(N_CTX_SEQ, 1, 2, HEADS, HEAD_D, HEAD_D))
```

```python
import functools
import math

import jax
import jax.numpy as jnp
import numpy as np
from jax import lax
from jax.experimental import pallas as pl
from jax.experimental.pallas import tpu as pltpu

F32 = jnp.float32
BF16 = jnp.bfloat16

D_MODEL = 4096
N_CTX_SEQ, CTX_T = 16, 256
N_LAT_SEQ, LAT_T = 2, 1024
M_CTX = N_CTX_SEQ * CTX_T
M_LAT = N_LAT_SEQ * LAT_T
M_TOK = M_CTX + M_LAT
GRID_W = 64
D_LRU = 2048
LRU_C = 8.0
HEADS = 16
HEAD_D = 128
DN_QK = HEADS * HEAD_D
CHUNK = 64
SUPER = 256
DN_HEADS_PER_STEP_CTX = 8
DN_HEADS_PER_STEP_LAT = 4
D_FF = 4 * D_MODEL
ALPHA = 2.0 ** 0.25
LN_EPS = 1e-5
RMS_EPS = 1e-6
POS_BASE = 10000.0
COL_LRU_X, COL_LRU_Y, COL_Q, COL_K, COL_V, COL_Z = 0, 2048, 4096, 6144, 8192, 10240
COL_BD = 12288
COL_GATE = 12352
N_MAIN = COL_BD
VMEM_LIMIT = 56 * 1024 * 1024
TM = 1024
TN = 512


def _cparams(n_axes):
    return pltpu.CompilerParams(dimension_semantics=("arbitrary",) * n_axes, vmem_limit_bytes=VMEM_LIMIT)


def _row_group(i, tm):
    n_ctx = M_CTX // tm
    return jnp.where(i < n_ctx, 0, 1 + (i - n_ctx) // (LAT_T // tm))


def _mod_spec(tm, k):
    return pl.BlockSpec((1, 1, D_MODEL), lambda i: (_row_group(i, tm), 0, k))


def _sigmoid(x):
    return 0.5 * (1.0 + jnp.tanh(0.5 * x))


def _silu(x):
    return x * _sigmoid(x)


def _softplus(x):
    return jnp.maximum(x, 0.0) + jnp.log1p(jnp.exp(-jnp.abs(x)))


def _dot(a, b):
    return jnp.dot(a, b, preferred_element_type=F32)


def _mod_kernel(c_ref, w_ref, b_ref, o_ref):
    s = _silu(c_ref[...]).astype(BF16)
    o_ref[...] = _dot(s, w_ref[...].astype(BF16)) + b_ref[...]


def _modulation(cc, w_mod, b_mod):
    tn = 512
    n = w_mod.shape[1]
    return pl.pallas_call(
        _mod_kernel,
        grid=(n // tn,),
        in_specs=[pl.BlockSpec((8, D_MODEL), lambda j: (0, 0)),
                  pl.BlockSpec((D_MODEL, tn), lambda j: (0, j)),
                  pl.BlockSpec((1, tn), lambda j: (0, j))],
        out_specs=pl.BlockSpec((8, tn), lambda j: (0, j)),
        out_shape=jax.ShapeDtypeStruct((8, n), F32),
        compiler_params=_cparams(1),
        name="modulation",
    )(cc, w_mod, b_mod)


def _token_specs(tm):
    n_ctx = M_CTX // tm
    n_pos = LAT_T // tm
    return [pl.BlockSpec((tm, D_MODEL), lambda i: (jnp.minimum(i, n_ctx - 1), 0)),
            pl.BlockSpec((tm, D_MODEL), lambda i: (jnp.maximum(i - n_ctx, 0), 0)),
            pl.BlockSpec((tm, D_MODEL), lambda i: (jnp.maximum(i - n_ctx, 0) % n_pos, 0))]


def _prep_kernel(xp_ref, xs_ref, pos_ref, sh_ref, sc_ref, wbd_ref, h_ref, bd_ref, wbd_b_ref, *, n_ctx_tiles):
    i = pl.program_id(0)
    sc = 1.0 + sc_ref[0]
    sh = sh_ref[0]

    @pl.when(i == 0)
    def _():
        wbd_b_ref[...] = wbd_ref[...].T.astype(BF16)

    def finish(x):
        h = (x * sc + sh).astype(BF16)
        h_ref[...] = h
        bd_ref[...] = _dot(h, wbd_b_ref[...])

    @pl.when(i < n_ctx_tiles)
    def _():
        finish(xp_ref[...])

    @pl.when(i >= n_ctx_tiles)
    def _():
        finish(xs_ref[...] + pos_ref[...])


def _prep(xp, xs, pos, mods3, w_in_t):
    tm = 256
    return pl.pallas_call(
        functools.partial(_prep_kernel, n_ctx_tiles=M_CTX // tm),
        grid=(M_TOK // tm,),
        in_specs=_token_specs(tm) + [_mod_spec(tm, 0), _mod_spec(tm, 1),
                                     pl.BlockSpec((pl.Element(HEAD_D), pl.Element(D_MODEL)), lambda i: (COL_BD, 0))],
        out_specs=[pl.BlockSpec((tm, D_MODEL), lambda i: (i, 0)), pl.BlockSpec((tm, HEAD_D), lambda i: (i, 0))],
        out_shape=[jax.ShapeDtypeStruct((M_TOK, D_MODEL), BF16), jax.ShapeDtypeStruct((M_TOK, HEAD_D), F32)],
        scratch_shapes=[pltpu.VMEM((D_MODEL, HEAD_D), BF16)],
        compiler_params=_cparams(1),
        name="prep",
    )(xp, xs, pos, mods3, mods3, w_in_t)


LHS_SLOTS = 3


def _mm_kernel(lhs_hbm, w_ref, *rest, relu2, sigmoid_bias, w_transposed, kc, tm, n_i, n_steps):
    if sigmoid_bias:
        bias_ref, o_ref, wb_ref, lbuf, sem = rest
    else:
        o_ref, wb_ref, lbuf, sem = rest
    k_dim = lbuf.shape[2]
    i = pl.program_id(1)
    step = pl.program_id(0) * n_i + i

    def lhs_copy(s, slot):
        rows = pl.ds(pl.multiple_of((s % n_i) * tm, tm), tm)
        return pltpu.make_async_copy(lhs_hbm.at[rows, :], lbuf.at[slot], sem.at[slot])

    @pl.when(step == 0)
    def _():
        lhs_copy(0, 0).start()
        lhs_copy(1, 1).start()

    @pl.when(step + 2 < n_steps)
    def _():
        lhs_copy(step + 2, (step + 2) % LHS_SLOTS).start()

    slot = step % LHS_SLOTS
    lhs_copy(step, slot).wait()

    def finish(acc):
        if relu2:
            acc = jnp.maximum(acc, 0.0)
            acc = acc * acc
        if sigmoid_bias:
            acc = _sigmoid(acc + bias_ref[...])
        o_ref[...] = acc.astype(o_ref.dtype)

    @pl.when(i == 0)
    def _():
        acc = None
        for c in range(k_dim // kc):
            ks = slice(c * kc, (c + 1) * kc)
            wb = (w_ref[:, ks].T if w_transposed else w_ref[ks, :]).astype(BF16)
            wb_ref[ks, :] = wb
            part = _dot(lbuf[slot, :, ks], wb)
            acc = part if acc is None else acc + part
        finish(acc)

    @pl.when(i > 0)
    def _():
        finish(_dot(lbuf[slot], wb_ref[...]))


def _mm(lhs, w, *, col0, n_cols, tn, out_dtype, relu2=False, sigmoid_bias=None, w_transposed=False, name):
    m, k = lhs.shape
    tm = TM
    n_j, n_i = n_cols // tn, m // tm
    if w_transposed:
        assert col0 % 8 == 0 and tn % 8 == 0
        w_spec = pl.BlockSpec((pl.Element(tn), pl.Element(k)), lambda j, i: (pl.multiple_of(col0 + j * tn, 8), 0))
    else:
        w_spec = pl.BlockSpec((k, tn), lambda j, i: (0, col0 // tn + j))
    in_specs = [pl.BlockSpec(memory_space=pl.ANY), w_spec]
    args = [lhs, w]
    if sigmoid_bias is not None:
        in_specs.append(pl.BlockSpec((1, tn), lambda j, i: (0, j)))
        args.append(sigmoid_bias)
    assert n_j * n_i >= LHS_SLOTS
    return pl.pallas_call(
        functools.partial(_mm_kernel, relu2=relu2, sigmoid_bias=sigmoid_bias is not None,
                          w_transposed=w_transposed, kc=512, tm=tm, n_i=n_i, n_steps=n_j * n_i),
        grid=(n_j, n_i),
        in_specs=in_specs,
        out_specs=pl.BlockSpec((tm, tn), lambda j, i: (i, j)),
        out_shape=jax.ShapeDtypeStruct((m, n_cols), out_dtype),
        scratch_shapes=[pltpu.VMEM((k, tn), BF16), pltpu.VMEM((LHS_SLOTS, tm, k), BF16),
                        pltpu.SemaphoreType.DMA((LHS_SLOTS,))],
        compiler_params=_cparams(2),
        name=name,
    )(*args)


def _down_kernel(lhs_ref, w_ref, o_ref, wb_ref, *, tm, kc):
    k = pl.program_id(1)
    i = pl.program_id(2)
    rows = pl.ds(pl.multiple_of(i * tm, tm), tm)
    tk = lhs_ref.shape[1]

    def accumulate(acc):
        @pl.when(k == 0)
        def _():
            o_ref[rows, :] = acc

        @pl.when(k > 0)
        def _():
            o_ref[rows, :] += acc

    @pl.when(i == 0)
    def _():
        acc = None
        for c in range(tk // kc):
            ks = slice(c * kc, (c + 1) * kc)
            wb = w_ref[ks, :].astype(BF16)
            wb_ref[ks, :] = wb
            part = _dot(lhs_ref[:, ks], wb)
            acc = part if acc is None else acc + part
        accumulate(acc)

    @pl.when(i > 0)
    def _():
        accumulate(_dot(lhs_ref[...], wb_ref[...]))


def _down(u, w_down):
    tk, tm = 4096, 1024
    m = u.shape[0]
    return pl.pallas_call(
        functools.partial(_down_kernel, tm=tm, kc=512),
        grid=(D_MODEL // TN, D_FF // tk, m // tm),
        in_specs=[pl.BlockSpec((tm, tk), lambda j, k, i: (i, k)),
                  pl.BlockSpec((tk, TN), lambda j, k, i: (k, j))],
        out_specs=pl.BlockSpec((m, TN), lambda j, k, i: (0, j), pipeline_mode=pl.Buffered(1)),
        out_shape=jax.ShapeDtypeStruct((m, D_MODEL), F32),
        scratch_shapes=[pltpu.VMEM((tk, TN), BF16)],
        compiler_params=_cparams(3),
        name="down_proj",
    )(u, w_down)


def _conv4(x, w):
    t = x.shape[0]
    row = lax.broadcasted_iota(jnp.int32, x.shape, 0)
    xm2 = jnp.where(row >= 2, pltpu.roll(x, 2, 0), 0.0)
    xm1 = jnp.where(row >= 1, pltpu.roll(x, 1, 0), 0.0)
    xp1 = jnp.where(row < t - 1, pltpu.roll(x, t - 1, 0), 0.0)
    return w[0:1] * xm2 + w[1:2] * xm1 + w[2:3] * x + w[3:4] * xp1


def _lru_kernel(x_ref, y_ref, cw_ref, cb_ref, gw_ref, gb_ref, lam_ref, h0_ref, o_ref, st_ref,
                af_ref, bf_ref, ab_ref, bb_ref, *, t_len, width):
    for n in range(width // HEAD_D):
        sl = slice(n * HEAD_D, (n + 1) * HEAD_D)
        xc = _conv4(x_ref[:, sl], cw_ref[:, sl]) + cb_ref[:, sl]
        xcb = xc.astype(BF16)
        for d, (a_ref, b_ref) in enumerate(((af_ref, bf_ref), (ab_ref, bb_ref))):
            t_r = jnp.tanh(_dot(xcb, (0.5 * gw_ref[d, 0, n]).astype(BF16)) + 0.5 * gb_ref[2 * d:2 * d + 1, sl])
            t_i = jnp.tanh(_dot(xcb, (0.5 * gw_ref[d, 1, n]).astype(BF16)) + 0.5 * gb_ref[2 * d + 1:2 * d + 2, sl])
            c = (-0.5 * LRU_C) * _softplus(-lam_ref[d:d + 1, sl])
            log_a = c + c * t_r
            a = jnp.exp(log_a)
            half_mult = jnp.sqrt((0.25 + 0.25 * (a * a)) * jnp.tanh(-log_a))
            a_ref[:, sl] = a
            b_ref[:, sl] = (half_mult * xc) * (1.0 + t_i)

    n_tiles = t_len // 8
    rowi = lax.broadcasted_iota(jnp.int32, (8, width), 0)

    def body(g, carry):
        cf, cb = carry
        r0 = pl.multiple_of(g * 8, 8)
        a8 = af_ref[pl.ds(r0, 8), :]
        b8 = bf_ref[pl.ds(r0, 8), :]
        for dd in (1, 2, 4):
            m = rowi >= dd
            a_sh = jnp.where(m, pltpu.roll(a8, dd, 0), 1.0)
            b_sh = jnp.where(m, pltpu.roll(b8, dd, 0), 0.0)
            b8 = a8 * b_sh + b8
            a8 = a8 * a_sh
        h8 = a8 * cf + b8
        bf_ref[pl.ds(r0, 8), :] = h8
        cf = h8[7:8, :]

        r1 = pl.multiple_of((n_tiles - 1 - g) * 8, 8)
        a8 = ab_ref[pl.ds(r1, 8), :]
        b8 = bb_ref[pl.ds(r1, 8), :]
        for dd in (1, 2, 4):
            m = rowi < 8 - dd
            a_sh = jnp.where(m, pltpu.roll(a8, 8 - dd, 0), 1.0)
            b_sh = jnp.where(m, pltpu.roll(b8, 8 - dd, 0), 0.0)
            b8 = a8 * b_sh + b8
            a8 = a8 * a_sh
        h8 = a8 * cb + b8
        bb_ref[pl.ds(r1, 8), :] = h8
        cb = h8[0:1, :]
        return cf, cb

    cf, cb = lax.fori_loop(0, n_tiles, body, (h0_ref[0, 0:1, :], h0_ref[0, 1:2, :]))
    st_ref[0, 0:1, :] = cf
    st_ref[0, 1:2, :] = cb
    y = y_ref[...]
    gelu = 0.5 * y * (1.0 + jnp.tanh(math.sqrt(2.0 / math.pi) * (y + 0.044715 * (y * y * y))))
    o_ref[...] = ((bf_ref[...] + bb_ref[...]) * gelu).astype(BF16)


def _lru(p_main, conv_w, conv_b, gate_w, gate_b4, lam, h0, *, t_len, row_block0, n_seq):
    width = 1024
    nblk = width // HEAD_D
    ncb = D_LRU // width
    return pl.pallas_call(
        functools.partial(_lru_kernel, t_len=t_len, width=width),
        grid=(n_seq, ncb),
        in_specs=[pl.BlockSpec((t_len, width), lambda b, c: (b + row_block0, c)),
                  pl.BlockSpec((t_len, width), lambda b, c: (b + row_block0, c + COL_LRU_Y // width)),
                  pl.BlockSpec((4, width), lambda b, c: (0, c)),
                  pl.BlockSpec((1, width), lambda b, c: (0, c)),
                  pl.BlockSpec((2, 2, nblk, HEAD_D, HEAD_D), lambda b, c: (0, 0, c, 0, 0)),
                  pl.BlockSpec((4, width), lambda b, c: (0, c)),
                  pl.BlockSpec((2, width), lambda b, c: (0, c)),
                  pl.BlockSpec((1, 2, width), lambda b, c: (b, 0, c))],
        out_specs=[pl.BlockSpec((t_len, width), lambda b, c: (b, c)),
                   pl.BlockSpec((1, 2, width), lambda b, c: (b, 0, c))],
        out_shape=[jax.ShapeDtypeStruct((n_seq * t_len, D_LRU), BF16),
                   jax.ShapeDtypeStruct((n_seq, 2, D_LRU), F32)],
        scratch_shapes=[pltpu.VMEM((t_len, width), F32)] * 4,
        compiler_params=_cparams(2),
        name=f"rg_lru_t{t_len}",
    )(p_main, p_main, conv_w, conv_b, gate_w, gate_b4, lam, h0)


N_LEVELS = 6
MASKB_INCL, MASKB_LEVEL0, MASKB_EYE = 0, 1, 1 + N_LEVELS


def _dn_masks():
    i = np.arange(SUPER)[:, None]
    j = np.arange(SUPER)[None, :]
    same = (i // CHUNK) == (j // CHUNK)
    out = np.zeros((2, MASKB_EYE + 1, SUPER, SUPER), np.float32)
    for d in range(2):
        before = (j > i) if d == 1 else (j < i)
        out[d, MASKB_INCL] = same & (before | (i == j))
        s = 1
        for lv in range(N_LEVELS):
            blk = (i // (2 * s)) == (j // (2 * s))
            il, jl = i % (2 * s), j % (2 * s)
            out[d, MASKB_LEVEL0 + lv] = blk & (((il < s) & (jl >= s)) if d == 1 else ((il >= s) & (jl < s)))
            s *= 2
        out[d, MASKB_EYE] = (i == j)
    return out


def _dn_block(chains, g_rows, m_ref, mb_ref, s_ref, vn_ref):
    nt = (((1,), (1,)), ((), ()))
    tn = (((0,), (0,)), ((), ()))
    n_chunks = SUPER // CHUNK
    gam_all = []
    for d in range(2):
        incl_b = mb_ref[d, MASKB_INCL]
        g_hi = g_rows[d].astype(BF16)
        r = g_rows[d] - g_hi.astype(F32)
        g_mid = r.astype(BF16)
        g_lo = (r - g_mid.astype(F32)).astype(BF16)
        gam_all.append(_dot(incl_b, g_hi) + (_dot(incl_b, g_mid) + _dot(incl_b, g_lo)))
    lane = lax.broadcasted_iota(jnp.int32, (SUPER, HEAD_D), 1)
    st = []
    for q, k, v, beta_b, g_lane, d, idx in chains:
        gam_col = jnp.sum(jnp.where(lane == g_lane, gam_all[d], 0.0), axis=1, keepdims=True)
        gam1 = jnp.broadcast_to(gam_col, (SUPER, HEAD_D))
        kb = k * beta_b
        kbf = k.astype(BF16)
        kk = lax.dot_general(kb.astype(BF16), kbf, nt, preferred_element_type=F32)
        qk = lax.dot_general(q.astype(BF16), kbf, nt, preferred_element_type=F32)
        st.append(dict(q=q, k=k, v=v, beta_b=beta_b, d=d, idx=idx, gam1=gam1, gam_col=gam_col, kb=kb, kk=kk, qk=qk))
    for c in st:
        d = c["d"]
        gam = jnp.broadcast_to(c.pop("gam_col"), (SUPER, SUPER))
        decay = jnp.exp(jnp.minimum(gam - gam.T, 0.0)) * m_ref[d]
        c["lmat"] = (c.pop("kk") * decay).astype(BF16)
        c["attn"] = (c.pop("qk") * decay).astype(BF16)
        c["minv"] = mb_ref[d, MASKB_EYE] - c["lmat"] * mb_ref[d, MASKB_LEVEL0]

    for lv in range(1, N_LEVELS):
        for c in st:
            c["x"] = _dot(c["lmat"] * mb_ref[c["d"], MASKB_LEVEL0 + lv], c["minv"]).astype(BF16)
        for c in st:
            c["minv"] = c["minv"] - _dot(c["minv"], c.pop("x")).astype(BF16)
    for c in st:
        eg = jnp.exp(c["gam1"])
        rhs = jnp.concatenate([c["v"] * c["beta_b"], c["kb"] * eg], axis=1)
        uw = _dot(c.pop("minv"), rhs.astype(BF16))
        c["u"], c["w"] = uw[:, :HEAD_D], uw[:, HEAD_D:].astype(BF16)
        c["q_dec"] = (c["q"] * eg).astype(BF16)
        lasts = []
        for n in range(n_chunks):
            r = n * CHUNK if c["d"] == 1 else n * CHUNK + CHUNK - 1
            lasts.append(jnp.broadcast_to(c["gam1"][r:r + 1, :], (CHUNK, HEAD_D)))
        c["g_last"] = jnp.concatenate(lasts, axis=0)
        c["k_dec"] = (c["k"] * jnp.exp(c["g_last"] - c["gam1"])).astype(BF16)
        c["outs"] = [None] * n_chunks
        vn_ref[c["idx"]] = jnp.zeros((SUPER, HEAD_D), F32)

    for step in range(n_chunks):
        for c in st:
            n = n_chunks - 1 - step if c["d"] == 1 else step
            rows = slice(n * CHUNK, (n + 1) * CHUNK)
            c["s"] = s_ref[c["idx"]]
            c["sb"] = c["s"].astype(BF16)
            c["v_new"] = c["u"][rows] - _dot(c["w"][rows], c["sb"])
            vn_ref[c["idx"], rows, :] = c["v_new"]
        for c in st:
            n = n_chunks - 1 - step if c["d"] == 1 else step
            rows = slice(n * CHUNK, (n + 1) * CHUNK)
            c["outs"][n] = (_dot(c["q_dec"][rows], c["sb"])
                            + _dot(c["attn"][rows], vn_ref[c["idx"]].astype(BF16)))
            inc = lax.dot_general(c["k_dec"][rows], c["v_new"].astype(BF16), tn, preferred_element_type=F32)
            s_ref[c["idx"]] = c["s"] * jnp.exp(c["g_last"][n * CHUNK:n * CHUNK + 1, :]) + inc
    return [jnp.concatenate(c["outs"], axis=0) for c in st]


def _gdn_kernel(*refs, t_len, use_state, emit_state, hps):
    (q_ref, k_ref, v_ref, z_ref, bd_ref, cwq_ref, cwk_ref, cwv_ref, alog_ref, dt_ref, nw_ref,
     m_ref, mb_ref), refs = refs[:13], refs[13:]
    if use_state:
        s0_ref, refs = refs[0], refs[1:]
    o_ref, refs = refs[0], refs[1:]
    if emit_state:
        st_ref, refs = refs[0], refs[1:]
    qs_ref, ks_ref, vs_ref, bet_ref, gall_ref, acc_ref, s_ref, vn_ref = refs

    head0 = pl.program_id(1) * hps

    def l2n(x):
        return x * lax.rsqrt(jnp.sum(x * x, axis=-1, keepdims=True) + RMS_EPS)

    bd = bd_ref[...]
    beta_all = _sigmoid(bd)
    gall_ref[...] = -jnp.exp(alog_ref[...]) * _softplus(bd + dt_ref[...])
    lane = lax.broadcasted_iota(jnp.int32, bd.shape, 1)
    for hh in range(hps):
        sl = slice(hh * HEAD_D, (hh + 1) * HEAD_D)
        qs_ref[:, sl] = l2n(_silu(_conv4(q_ref[:, sl], cwq_ref[:, sl]))) * (HEAD_D ** -0.5)
        ks_ref[:, sl] = l2n(_silu(_conv4(k_ref[:, sl], cwk_ref[:, sl])))
        vs_ref[:, sl] = _silu(_conv4(v_ref[:, sl], cwv_ref[:, sl]))
        for d in range(2):
            col = d * HEADS + head0 + hh
            beta = jnp.sum(jnp.where(lane == col, beta_all, 0.0), axis=1, keepdims=True)
            bet_ref[d, :, sl] = jnp.broadcast_to(beta, (t_len, HEAD_D))
            if use_state:
                s_ref[2 * hh + d] = s0_ref[0, d, hh]
            else:
                s_ref[2 * hh + d] = jnp.zeros((HEAD_D, HEAD_D), F32)

    n_blocks = t_len // SUPER

    def block_step(nb):
        chains, dests, g_rows = [], [], []
        for d in range(2):
            blk = nb if d == 0 else n_blocks - 1 - nb
            g_rows.append(gall_ref[pl.ds(pl.multiple_of(blk * SUPER, SUPER), SUPER), :])
        for hh in range(hps):
            sl = slice(hh * HEAD_D, (hh + 1) * HEAD_D)
            for d in range(2):
                blk = nb if d == 0 else n_blocks - 1 - nb
                rows = pl.ds(pl.multiple_of(blk * SUPER, SUPER), SUPER)
                chains.append((qs_ref[rows, sl], ks_ref[rows, sl], vs_ref[rows, sl], bet_ref[d, rows, sl],
                               2 * HEADS + d * HEADS + head0 + hh, d, 2 * hh + d))
                dests.append((d, rows, sl))
        for (d, rows, sl), out in zip(dests, _dn_block(chains, g_rows, m_ref, mb_ref, s_ref, vn_ref)):
            acc_ref[d, rows, sl] = out

    if n_blocks == 1:
        block_step(0)
    else:
        def loop_body(nb, carry):
            block_step(nb)
            return carry
        lax.fori_loop(0, n_blocks, loop_body, 0)

    for hh in range(hps):
        sl = slice(hh * HEAD_D, (hh + 1) * HEAD_D)
        if emit_state:
            for d in range(2):
                st_ref[0, d, hh] = s_ref[2 * hh + d]
        o = acc_ref[0, :, sl] + acc_ref[1, :, sl]
        o = o * lax.rsqrt(jnp.mean(o * o, axis=-1, keepdims=True) + RMS_EPS) * nw_ref[...]
        o_ref[:, sl] = (o * _silu(z_ref[:, sl])).astype(BF16)


def _gdn(p_main, p_bd, conv_w, alog_vec, dt_vec, norm_w, mask_incl, masks_b, s0, *,
         t_len, row_block0, n_seq, emit_state, hps):
    use_state = s0 is not None
    wid = hps * HEAD_D
    cb = lambda col: col // wid
    buffering = dict(pipeline_mode=pl.Buffered(1)) if t_len > CTX_T else {}
    tok = lambda col: pl.BlockSpec((t_len, wid), lambda b, h: (b + row_block0, h + cb(col)), **buffering)
    cws = lambda col: pl.BlockSpec((4, wid), lambda b, h: (0, h + cb(col - COL_Q)))
    vec = pl.BlockSpec((1, HEAD_D), lambda b, h: (0, 0))
    state_spec = pl.BlockSpec((1, 2, hps, HEAD_D, HEAD_D), lambda b, h: (b, 0, h, 0, 0))
    in_specs = [tok(COL_Q), tok(COL_K), tok(COL_V), tok(COL_Z),
                pl.BlockSpec((t_len, HEAD_D), lambda b, h: (b + row_block0, 0)),
                cws(COL_Q), cws(COL_K), cws(COL_V), vec, vec, vec,
                pl.BlockSpec(mask_incl.shape, lambda b, h: (0, 0, 0)),
                pl.BlockSpec(masks_b.shape, lambda b, h: (0, 0, 0, 0))]
    args = [p_main, p_main, p_main, p_main, p_bd, conv_w, conv_w, conv_w, alog_vec, dt_vec, norm_w,
            mask_incl, masks_b]
    if use_state:
        in_specs.append(state_spec)
        args.append(s0)
    out_specs = [pl.BlockSpec((t_len, wid), lambda b, h: (b, h))]
    out_shape = [jax.ShapeDtypeStruct((n_seq * t_len, DN_QK), BF16)]
    if emit_state:
        out_specs.append(state_spec)
        out_shape.append(jax.ShapeDtypeStruct((n_seq, 2, HEADS, HEAD_D, HEAD_D), F32))
    scratch = ([pltpu.VMEM((t_len, wid), F32)] * 3
               + [pltpu.VMEM((2, t_len, wid), F32),
                  pltpu.VMEM((t_len, HEAD_D), F32),
                  pltpu.VMEM((2, t_len, wid), F32)]
               + [pltpu.VMEM((2 * hps, HEAD_D, HEAD_D), F32),
                  pltpu.VMEM((2 * hps, SUPER, HEAD_D), F32)])
    return pl.pallas_call(
        functools.partial(_gdn_kernel, t_len=t_len, use_state=use_state, emit_state=emit_state, hps=hps),
        grid=(n_seq, HEADS // hps),
        in_specs=in_specs, out_specs=out_specs, out_shape=out_shape,
        scratch_shapes=scratch,
        compiler_params=_cparams(2),
        name=f"gated_delta_t{t_len}",
    )(*args)


def _merge_kernel(ac_ref, al_ref, bc_ref, bl_ref, wl_ref, wd_ref, g0_ref, g1_ref, o_ref,
                  wlb_ref, wdb_ref, *, n_ctx_tiles, kc):
    i = pl.program_id(1)

    def finish(pl_, pd_):
        o_ref[...] = (g0_ref[...].astype(F32) * pl_ + g1_ref[...].astype(F32) * pd_).astype(BF16)

    @pl.when(i == 0)
    def _():
        accs = []
        for a_ref, w_ref, wb_ref in ((ac_ref, wl_ref, wlb_ref), (bc_ref, wd_ref, wdb_ref)):
            acc = None
            for c in range(w_ref.shape[0] // kc):
                ks = slice(c * kc, (c + 1) * kc)
                wb = w_ref[ks, :].astype(BF16)
                wb_ref[ks, :] = wb
                part = _dot(a_ref[:, ks], wb)
                acc = part if acc is None else acc + part
            accs.append(acc)
        finish(*accs)

    @pl.when((i > 0) & (i < n_ctx_tiles))
    def _():
        finish(_dot(ac_ref[...], wlb_ref[...]), _dot(bc_ref[...], wdb_ref[...]))

    @pl.when(i >= n_ctx_tiles)
    def _():
        finish(_dot(al_ref[...], wlb_ref[...]), _dot(bl_ref[...], wdb_ref[...]))


def _merge(lru_ctx, lru_lat, dn_ctx, dn_lat, w_lru, w_dn, gates):
    tm, tn = 512, 1024
    nj = D_MODEL // tn
    n_ctx = M_CTX // tm
    ctx = pl.BlockSpec((tm, D_LRU), lambda j, i: (jnp.minimum(i, n_ctx - 1), 0))
    lat = pl.BlockSpec((tm, D_LRU), lambda j, i: (jnp.maximum(i - n_ctx, 0), 0))
    wspec = pl.BlockSpec((D_LRU, tn), lambda j, i: (0, j), pipeline_mode=pl.Buffered(1))
    return pl.pallas_call(
        functools.partial(_merge_kernel, n_ctx_tiles=n_ctx, kc=512),
        grid=(nj, M_TOK // tm),
        in_specs=[ctx, lat, ctx, lat, wspec, wspec,
                  pl.BlockSpec((tm, tn), lambda j, i: (i, j)),
                  pl.BlockSpec((tm, tn), lambda j, i: (i, j + nj))],
        out_specs=pl.BlockSpec((tm, tn), lambda j, i: (i, j)),
        out_shape=jax.ShapeDtypeStruct((M_TOK, D_MODEL), BF16),
        scratch_shapes=[pltpu.VMEM((D_LRU, tn), BF16), pltpu.VMEM((DN_QK, tn), BF16)],
        compiler_params=_cparams(2),
        name="branch_merge",
    )(lru_ctx, lru_lat, dn_ctx, dn_lat, w_lru, w_dn, gates, gates)


def _layer_norm(x, g, b):
    mu = jnp.mean(x, axis=-1, keepdims=True)
    xc = x - mu
    var = jnp.mean(xc * xc, axis=-1, keepdims=True)
    return xc * lax.rsqrt(var + LN_EPS) * g + b


def _ln1_kernel(xp_ref, xs_ref, pos_ref, mix_ref, gm_ref, shf_ref, scf_ref, g_ref, b_ref, x1_ref, h2_ref,
                *, n_ctx_tiles):
    i = pl.program_id(0)

    def finish(x):
        x1 = _layer_norm(ALPHA * x + gm_ref[0] * mix_ref[...], g_ref[...], b_ref[...])
        x1_ref[...] = x1
        h2_ref[...] = (x1 * (1.0 + scf_ref[0]) + shf_ref[0]).astype(BF16)

    @pl.when(i < n_ctx_tiles)
    def _():
        finish(xp_ref[...])

    @pl.when(i >= n_ctx_tiles)
    def _():
        finish(xs_ref[...] + pos_ref[...])


def _ln1(xp, xs, pos, mix, mods3, g, b):
    tm = 256
    row = pl.BlockSpec((tm, D_MODEL), lambda i: (i, 0))
    vec = pl.BlockSpec((1, D_MODEL), lambda i: (0, 0))
    return pl.pallas_call(
        functools.partial(_ln1_kernel, n_ctx_tiles=M_CTX // tm),
        grid=(M_TOK // tm,),
        in_specs=_token_specs(tm) + [row, _mod_spec(tm, 2), _mod_spec(tm, 3), _mod_spec(tm, 4), vec, vec],
        out_specs=[row, row],
        out_shape=[jax.ShapeDtypeStruct((M_TOK, D_MODEL), F32), jax.ShapeDtypeStruct((M_TOK, D_MODEL), BF16)],
        compiler_params=_cparams(1),
        name="residual_ln1",
    )(xp, xs, pos, mix, mods3, mods3, mods3, g, b)


def _ln2_kernel(x1_ref, ffn_ref, gf_ref, g_ref, b_ref, y_ref):
    y_ref[...] = _layer_norm(ALPHA * x1_ref[...] + gf_ref[0] * ffn_ref[...], g_ref[...], b_ref[...])


def _ln2(x1, ffn, mods3, g, b, *, row0, n_rows):
    tm = 256
    t0 = row0 // tm
    row = pl.BlockSpec((tm, D_MODEL), lambda i: (i + t0, 0))
    vec = pl.BlockSpec((1, D_MODEL), lambda i: (0, 0))
    return pl.pallas_call(
        _ln2_kernel,
        grid=(n_rows // tm,),
        in_specs=[row, row, pl.BlockSpec((1, 1, D_MODEL), lambda i: (_row_group(i + t0, tm), 0, 5)), vec, vec],
        out_specs=pl.BlockSpec((tm, D_MODEL), lambda i: (i, 0)),
        out_shape=jax.ShapeDtypeStruct((n_rows, D_MODEL), F32),
        compiler_params=_cparams(1),
        name=f"residual_ln2_r{row0}",
    )(x1, ffn, mods3, g, b)


def _grid_pos_embed():
    t = np.arange(LAT_T)
    quarter = D_MODEL // 4
    omega = 1.0 / (POS_BASE ** (np.arange(quarter, dtype=np.float64) / quarter))
    er = (t // GRID_W)[:, None] * omega
    ec = (t % GRID_W)[:, None] * omega
    return jnp.asarray(np.concatenate([np.sin(er), np.cos(er), np.sin(ec), np.cos(ec)], axis=-1), dtype=F32)


def _lane_vec(v2x16):
    return jnp.zeros((1, HEAD_D), F32).at[0, 2 * HEADS:4 * HEADS].set(v2x16.reshape(-1))


def kernel(x_prompt, x_sample, state_lru, state_dn, c, c_ctx, w_mod, b_mod, w_in, lru_conv_w, lru_conv_b, lru_gate_w, lru_gate_b, lru_lambda, dn_conv_w, dn_a_log, dn_dt_bias, dn_norm_w, b_branch, w_lru_proj, w_dn_proj, w_o, ln1_g, ln1_b, w_up, w_down, ln2_g, ln2_b):
    xp = x_prompt.reshape(M_CTX, D_MODEL)
    xs = x_sample.reshape(M_LAT, D_MODEL)
    pos = _grid_pos_embed()

    cc = jnp.zeros((8, D_MODEL), F32).at[0].set(c_ctx).at[1:1 + N_LAT_SEQ].set(c)
    mods = _modulation(cc, w_mod[0], b_mod)
    mods3 = mods.reshape(8, 1, 6 * D_MODEL)

    w_in_t = jnp.swapaxes(w_in, 1, 2)[0]
    h, p_bd = _prep(xp, xs, pos, mods3, w_in_t)
    p_main = _mm(h, w_in_t, col0=0, n_cols=N_MAIN, tn=TN, out_dtype=F32, w_transposed=True, name="in_proj_main")
    gates = _mm(h, w_in_t, col0=COL_GATE, n_cols=2 * D_MODEL, tn=TN, out_dtype=BF16, w_transposed=True,
                sigmoid_bias=b_branch.reshape(1, 2 * D_MODEL), name="in_proj_gates")

    gate_b4 = lru_gate_b[0].reshape(4, D_LRU)
    lru_args = (lru_conv_w[0], lru_conv_b, lru_gate_w[0], gate_b4, lru_lambda[0])
    lru_ctx, lru_state = _lru(p_main, *lru_args, jnp.zeros((N_CTX_SEQ, 2, D_LRU), F32),
                              t_len=CTX_T, row_block0=0, n_seq=N_CTX_SEQ)
    lru_lat, _ = _lru(p_main, *lru_args, state_lru[:, 0],
                      t_len=LAT_T, row_block0=M_CTX // LAT_T, n_seq=N_LAT_SEQ)

    masks = _dn_masks()
    dn_args = (dn_conv_w[0], _lane_vec(dn_a_log[0]), _lane_vec(dn_dt_bias[0]), dn_norm_w,
               jnp.asarray(masks[:, MASKB_INCL]), jnp.asarray(masks, dtype=BF16))
    dn_ctx, dn_state = _gdn(p_main, p_bd, *dn_args, None,
                            t_len=CTX_T, row_block0=0, n_seq=N_CTX_SEQ, emit_state=True, hps=DN_HEADS_PER_STEP_CTX)
    (dn_lat,) = _gdn(p_main, p_bd, *dn_args, state_dn[:, 0],
                     t_len=LAT_T, row_block0=M_CTX // LAT_T, n_seq=N_LAT_SEQ, emit_state=False, hps=DN_HEADS_PER_STEP_LAT)

    merged = _merge(lru_ctx, lru_lat, dn_ctx, dn_lat, w_lru_proj[0], w_dn_proj[0], gates)
    mix = _mm(merged, w_o[0], col0=0, n_cols=D_MODEL, tn=TN, out_dtype=F32, name="out_proj")

    x1, h2 = _ln1(xp, xs, pos, mix, mods3, ln1_g, ln1_b)
    u = _mm(h2, w_up[0], col0=0, n_cols=D_FF, tn=TN, out_dtype=BF16, relu2=True, name="up_proj")
    ffn = _down(u, w_down[0])

    y_ctx = _ln2(x1, ffn, mods3, ln2_g, ln2_b, row0=0, n_rows=M_CTX)
    y_lat = _ln2(x1, ffn, mods3, ln2_g, ln2_b, row0=M_CTX, n_rows=M_LAT)

    return (y_ctx.reshape(x_prompt.shape), y_lat.reshape(x_sample.shape),
            lru_state.reshape(N_CTX_SEQ, 1, 2, D_LRU),
            dn_state.reshape(N_CTX_SEQ, 1, 2, HEADS, HEAD_D, HEAD_D))
```
